```python
import math
import jax, jax.numpy as jnp
from jax import lax
import numpy as np

D_MODEL = 2048
BATCH = 4
SEQ = 4096
DEPTH = 2

CONV_WIDTH = D_MODEL // 2
CONV_K = 3
MLA_V = 128
MLA_HEADS = (D_MODEL // 2) // MLA_V
MLA_NOPE = 128
MLA_ROPE = 64
MLA_Q_RANK = D_MODEL // 4
MLA_KV_RANK = D_MODEL // 8
ROPE_THETA = 10000.0
EVEN_IN_WIDTH = 3 * CONV_WIDTH + MLA_Q_RANK + MLA_KV_RANK + MLA_ROPE
EVEN_SPLITS = (CONV_WIDTH, 2 * CONV_WIDTH, 3 * CONV_WIDTH,
               3 * CONV_WIDTH + MLA_Q_RANK, 3 * CONV_WIDTH + MLA_Q_RANK + MLA_KV_RANK)
EVEN_MIX_WIDTH = CONV_WIDTH + MLA_HEADS * MLA_V
DIFF_HEAD_DIM = 128
DIFF_HEADS = D_MODEL // (2 * DIFF_HEAD_DIM)
N_GROUPS = 4
EXPERTS_PER_GROUP = 8
N_EXPERTS = N_GROUPS * EXPERTS_PER_GROUP
EXPERT_FF = D_MODEL // 4
TOP_K = 2

Q_BLOCK = 128
NORM_EPS = 1e-6

kernel_name = "hybrid_shortconv_mla_diffattn_hmoe"


def _rms_norm(x, g):
    xf = x.astype(jnp.float32)
    y = xf * lax.rsqrt(jnp.mean(xf * xf, axis=-1, keepdims=True) + NORM_EPS)
    return (y * g.astype(jnp.float32)).astype(x.dtype)


def _to_blocks(t):
    b, s = t.shape[:2]
    return jnp.moveaxis(t.reshape((b, s // Q_BLOCK, Q_BLOCK) + t.shape[2:]), 1, 0)


def _from_blocks(t):
    t = jnp.moveaxis(t, 0, 1)
    return t.reshape((t.shape[0], t.shape[1] * t.shape[2]) + t.shape[3:])


def _rope_tables(s):
    pos = jnp.arange(s, dtype=jnp.float32)
    inv_freq = ROPE_THETA ** (-jnp.arange(0, MLA_ROPE, 2, dtype=jnp.float32) / MLA_ROPE)
    ang = pos[:, None] * inv_freq[None, :]
    return jnp.cos(ang), jnp.sin(ang)


def _rope(t, cos, sin):
    t1, t2 = jnp.split(t.astype(jnp.float32), 2, axis=-1)
    return jnp.concatenate([t1 * cos - t2 * sin, t1 * sin + t2 * cos], axis=-1).astype(t.dtype)


def _alibi_slopes(n):
    return (2.0 ** (-8.0 * np.arange(1, n + 1) / n)).astype(np.float32)


def _causal_depthwise_conv(u, w):
    return lax.conv_general_dilated(
        u, w[:, None, :].astype(u.dtype), window_strides=(1,), padding=[(CONV_K - 1, 0)],
        dimension_numbers=("NWC", "WIO", "NWC"), feature_group_count=u.shape[-1])


def _mla_attention(q_nope, q_rope, k_nope, k_rope, v):
    s = q_nope.shape[1]
    scale = (MLA_NOPE + MLA_ROPE) ** -0.5
    k_pos = jnp.arange(s)

    def block(args):
        qn, qr, i = args
        q_pos = i * Q_BLOCK + jnp.arange(Q_BLOCK)
        scores = (jnp.einsum("bqhd,bkhd->bhqk", qn, k_nope)
                  + jnp.einsum("bqhr,bkr->bhqk", qr, k_rope)).astype(jnp.float32) * scale
        scores = jnp.where(k_pos[None, :] <= q_pos[:, None], scores, -jnp.inf)
        p = jax.nn.softmax(scores, axis=-1).astype(v.dtype)
        return jnp.einsum("bhqk,bkhe->bqhe", p, v)

    out = lax.map(block, (_to_blocks(q_nope), _to_blocks(q_rope), jnp.arange(s // Q_BLOCK)))
    return _from_blocks(out)


def _diff_attention(q, k, v, lam):
    s = q.shape[1]
    scale = DIFF_HEAD_DIM ** -0.5
    slopes = jnp.asarray(_alibi_slopes(DIFF_HEADS))
    k_pos = jnp.arange(s)

    def block(args):
        qb, i = args
        q_pos = i * Q_BLOCK + jnp.arange(Q_BLOCK)
        dist = (q_pos[:, None] - k_pos[None, :]).astype(jnp.float32)
        bias = -slopes[:, None, None] * dist
        scores = jnp.einsum("bqhjd,bkhjd->bhjqk", qb, k).astype(jnp.float32) * scale + bias[None, :, None]
        scores = jnp.where(dist >= 0.0, scores, -jnp.inf)
        p = jax.nn.softmax(scores, axis=-1)
        a = p[:, :, 0] - lam * p[:, :, 1]
        return jnp.einsum("bhqk,bkhe->bqhe", a.astype(v.dtype), v)

    out = lax.map(block, (_to_blocks(q), jnp.arange(s // Q_BLOCK)))
    return _from_blocks(out)


def _even_mixer(x, g_mix, w_in, conv_w, g_q_lat, w_uq, g_kv_lat, w_ukv, w_out):
    b, s, _ = x.shape
    h = _rms_norm(x, g_mix)
    gate_b, gate_c, u, c_q, c_kv, k_r = jnp.split(h @ w_in, EVEN_SPLITS, axis=-1)
    y_conv = gate_b * _causal_depthwise_conv(gate_c * u, conv_w)
    cos, sin = _rope_tables(s)
    q = (_rms_norm(c_q, g_q_lat) @ w_uq).reshape(b, s, MLA_HEADS, MLA_NOPE + MLA_ROPE)
    q_nope = q[..., :MLA_NOPE]
    q_rope = _rope(q[..., MLA_NOPE:], cos[:, None, :], sin[:, None, :])
    kv = (_rms_norm(c_kv, g_kv_lat) @ w_ukv).reshape(b, s, MLA_HEADS, MLA_NOPE + MLA_V)
    k_nope, v = kv[..., :MLA_NOPE], kv[..., MLA_NOPE:]
    k_rope = _rope(k_r, cos, sin)
    y_mla = _mla_attention(q_nope, q_rope, k_nope, k_rope, v).reshape(b, s, MLA_HEADS * MLA_V)
    return jnp.concatenate([y_conv, y_mla], axis=-1) @ w_out


def _odd_mixer(x, g_mix, w_qkv, lam_q1, lam_k1, lam_q2, lam_k2, g_subln, w_out, layer_idx):
    b, s, _ = x.shape
    h = _rms_norm(x, g_mix)
    q, k, v = jnp.split(h @ w_qkv, 3, axis=-1)
    q = q.reshape(b, s, DIFF_HEADS, 2, DIFF_HEAD_DIM)
    k = k.reshape(b, s, DIFF_HEADS, 2, DIFF_HEAD_DIM)
    v = v.reshape(b, s, DIFF_HEADS, 2 * DIFF_HEAD_DIM)
    lam_init = 0.8 - 0.6 * math.exp(-0.3 * layer_idx)
    f32 = jnp.float32
    lam = (jnp.exp(jnp.sum(lam_q1.astype(f32) * lam_k1.astype(f32)))
           - jnp.exp(jnp.sum(lam_q2.astype(f32) * lam_k2.astype(f32))) + lam_init)
    o = _diff_attention(q, k, v, lam)
    o = _rms_norm(o, g_subln) * (1.0 - lam_init)
    return o.reshape(b, s, DIFF_HEADS * 2 * DIFF_HEAD_DIM) @ w_out


def _hier_moe(h, w_group, b_group, w_expert, b_expert, w_up_gate, w_down):
    b, s, d = h.shape
    t = h.reshape(b * s, d)
    g_probs = jax.nn.softmax((t @ w_group + b_group).astype(jnp.float32), axis=-1)
    g_prob, g_idx = lax.top_k(g_probs, 1)
    e_logits = (t @ w_expert + b_expert).astype(jnp.float32).reshape(-1, N_GROUPS, EXPERTS_PER_GROUP)
    e_logits = jnp.take_along_axis(e_logits, g_idx[:, :, None], axis=1)[:, 0]
    top_logits, e_idx = lax.top_k(e_logits, TOP_K)
    weights = g_prob * jax.nn.softmax(top_logits, axis=-1)
    expert_id = g_idx * EXPERTS_PER_GROUP + e_idx
    combine = jnp.sum(jax.nn.one_hot(expert_id, N_EXPERTS, dtype=jnp.float32) * weights[..., None], axis=1)
    combine = combine.astype(t.dtype)
    out = jnp.zeros_like(t)
    for e in range(N_EXPERTS):
        gate, up = jnp.split(t @ w_up_gate[e], 2, axis=-1)
        out = out + combine[:, e:e + 1] * ((jax.nn.silu(gate) * up) @ w_down[e])
    return out.reshape(b, s, d)


def setup_inputs(seed: int = 0) -> dict:
    key = jax.random.key(seed)
    ks = iter(jax.random.split(key, 40))

    def w(shape, fan_in):
        return jax.random.normal(next(ks), shape, jnp.float32) * fan_in ** -0.5

    def gain(n):
        return 1.0 + 0.01 * jax.random.normal(next(ks), (n,), jnp.float32)

    def small(shape, scale):
        return scale * jax.random.normal(next(ks), shape, jnp.float32)

    inp = {}
    inp["x"] = jax.random.normal(next(ks), (BATCH, SEQ, D_MODEL), jnp.float32)
    inp["l0_g_mix"] = gain(D_MODEL)
    inp["l0_w_in"] = w((D_MODEL, EVEN_IN_WIDTH), D_MODEL)
    inp["l0_conv_w"] = w((CONV_K, CONV_WIDTH), CONV_K)
    inp["l0_g_q_lat"] = gain(MLA_Q_RANK)
    inp["l0_w_uq"] = w((MLA_Q_RANK, MLA_HEADS * (MLA_NOPE + MLA_ROPE)), MLA_Q_RANK)
    inp["l0_g_kv_lat"] = gain(MLA_KV_RANK)
    inp["l0_w_ukv"] = w((MLA_KV_RANK, MLA_HEADS * (MLA_NOPE + MLA_V)), MLA_KV_RANK)
    inp["l0_w_out"] = w((EVEN_MIX_WIDTH, D_MODEL), EVEN_MIX_WIDTH)
    inp["l0_g_ffn"] = gain(D_MODEL)
    inp["l0_w_group"] = w((D_MODEL, N_GROUPS), D_MODEL)
    inp["l0_b_group"] = small((N_GROUPS,), 0.01)
    inp["l0_w_expert"] = w((D_MODEL, N_EXPERTS), D_MODEL)
    inp["l0_b_expert"] = small((N_EXPERTS,), 0.01)
    inp["l0_w_up_gate"] = w((N_EXPERTS, D_MODEL, 2 * EXPERT_FF), D_MODEL)
    inp["l0_w_down"] = w((N_EXPERTS, EXPERT_FF, D_MODEL), EXPERT_FF)
    inp["l1_g_mix"] = gain(D_MODEL)
    inp["l1_w_qkv"] = w((D_MODEL, 3 * D_MODEL), D_MODEL)
    inp["l1_lam_q1"] = small((DIFF_HEAD_DIM,), 0.1)
    inp["l1_lam_k1"] = small((DIFF_HEAD_DIM,), 0.1)
    inp["l1_lam_q2"] = small((DIFF_HEAD_DIM,), 0.1)
    inp["l1_lam_k2"] = small((DIFF_HEAD_DIM,), 0.1)
    inp["l1_g_subln"] = gain(2 * DIFF_HEAD_DIM)
    inp["l1_w_out"] = w((DIFF_HEADS * 2 * DIFF_HEAD_DIM, D_MODEL), DIFF_HEADS * 2 * DIFF_HEAD_DIM)
    inp["l1_g_ffn"] = gain(D_MODEL)
    inp["l1_w_group"] = w((D_MODEL, N_GROUPS), D_MODEL)
    inp["l1_b_group"] = small((N_GROUPS,), 0.01)
    inp["l1_w_expert"] = w((D_MODEL, N_EXPERTS), D_MODEL)
    inp["l1_b_expert"] = small((N_EXPERTS,), 0.01)
    inp["l1_w_up_gate"] = w((N_EXPERTS, D_MODEL, 2 * EXPERT_FF), D_MODEL)
    inp["l1_w_down"] = w((N_EXPERTS, EXPERT_FF, D_MODEL), EXPERT_FF)
    inp["g_final"] = gain(D_MODEL)
    return inp


def reference(x,
              l0_g_mix, l0_w_in, l0_conv_w, l0_g_q_lat, l0_w_uq, l0_g_kv_lat, l0_w_ukv, l0_w_out,
              l0_g_ffn, l0_w_group, l0_b_group, l0_w_expert, l0_b_expert, l0_w_up_gate, l0_w_down,
              l1_g_mix, l1_w_qkv, l1_lam_q1, l1_lam_k1, l1_lam_q2, l1_lam_k2, l1_g_subln, l1_w_out,
              l1_g_ffn, l1_w_group, l1_b_group, l1_w_expert, l1_b_expert, l1_w_up_gate, l1_w_down,
              g_final):
    even_params = ((l0_g_mix, l0_w_in, l0_conv_w, l0_g_q_lat, l0_w_uq, l0_g_kv_lat, l0_w_ukv, l0_w_out),)
    odd_params = ((l1_g_mix, l1_w_qkv, l1_lam_q1, l1_lam_k1, l1_lam_q2, l1_lam_k2, l1_g_subln, l1_w_out),)
    ffn_params = ((l0_g_ffn, l0_w_group, l0_b_group, l0_w_expert, l0_b_expert, l0_w_up_gate, l0_w_down),
                  (l1_g_ffn, l1_w_group, l1_b_group, l1_w_expert, l1_b_expert, l1_w_up_gate, l1_w_down))
    for layer in range(DEPTH):
        if layer % 2 == 0:
            x = x + _even_mixer(x, *even_params[layer // 2])
        else:
            x = x + _odd_mixer(x, *odd_params[layer // 2], layer_idx=layer)
        g_ffn = ffn_params[layer][0]
        x = x + _hier_moe(_rms_norm(x, g_ffn), *ffn_params[layer][1:])
    return _rms_norm(x, g_final)
```

```python
import functools
import math

import jax
import jax.numpy as jnp
import numpy as np
from jax import lax
from jax.experimental import pallas as pl
from jax.experimental.pallas import tpu as pltpu

F32 = jnp.float32
BF16 = jnp.bfloat16

NORM_EPS = 1e-6
ROPE_THETA = 10000.0
LANES = 128
SUBLANES = 8
MOE_TILE = 256
NORM_ROWS = 256
DMA_UNROLL = 8
N_GROUPS = 4
TOP_K = 2
VMEM_LIMIT_BYTES = 52 * 1024 * 1024


def _cparams(*sem):
    return pltpu.CompilerParams(dimension_semantics=sem, vmem_limit_bytes=VMEM_LIMIT_BYTES)


def _rms(xf, g):
    ms = jnp.mean(xf * xf, axis=-1, keepdims=True)
    return xf * lax.rsqrt(ms + NORM_EPS) * g


def _norm_matmul_kernel(x_ref, g_ref, w_ref, o_ref, h_ref):
    @pl.when(pl.program_id(1) == 0)
    def _():
        rows = min(NORM_ROWS, x_ref.shape[0])
        for r0 in range(0, x_ref.shape[0], rows):
            h_ref[r0:r0 + rows, :] = _rms(x_ref[r0:r0 + rows, :], g_ref[...]).astype(BF16)

    o_ref[...] = jnp.dot(h_ref[...], w_ref[...], preferred_element_type=F32).astype(o_ref.dtype)


def _norm_matmul(x, g, w, out_dtype, tm, tn):
    n, k = x.shape
    nn = w.shape[1]
    tm, tn = min(tm, n), min(tn, nn)
    return pl.pallas_call(
        _norm_matmul_kernel,
        out_shape=jax.ShapeDtypeStruct((n, nn), out_dtype),
        grid=(n // tm, nn // tn),
        in_specs=[pl.BlockSpec((tm, k), lambda i, j: (i, 0)),
                  pl.BlockSpec((1, k), lambda i, j: (0, 0)),
                  pl.BlockSpec((k, tn), lambda i, j: (0, j))],
        out_specs=pl.BlockSpec((tm, tn), lambda i, j: (i, j)),
        scratch_shapes=[pltpu.VMEM((tm, k), BF16)],
        compiler_params=_cparams("parallel", "arbitrary"),
        name="norm_matmul",
    )(x, g.reshape(1, k), w)


def _matmul_kernel(x_ref, w_ref, o_ref):
    o_ref[...] = jnp.dot(x_ref[...], w_ref[...], preferred_element_type=F32).astype(o_ref.dtype)


def _matmul(x, w, out_dtype, tm, tn):
    n, k = x.shape
    nn = w.shape[1]
    tm, tn = min(tm, n), min(tn, nn)
    return pl.pallas_call(
        _matmul_kernel,
        out_shape=jax.ShapeDtypeStruct((n, nn), out_dtype),
        grid=(n // tm, nn // tn),
        in_specs=[pl.BlockSpec((tm, k), lambda i, j: (i, 0)),
                  pl.BlockSpec((k, tn), lambda i, j: (0, j))],
        out_specs=pl.BlockSpec((tm, tn), lambda i, j: (i, j)),
        compiler_params=_cparams("parallel", "arbitrary"),
        name="matmul",
    )(x, w)


def _outproj_kernel(ya_ref, yb_ref, w_ref, x_ref, o_ref):
    ka = ya_ref.shape[1]
    acc = jnp.dot(ya_ref[...], w_ref[:ka, :], preferred_element_type=F32)
    acc = acc + jnp.dot(yb_ref[...], w_ref[ka:, :], preferred_element_type=F32)
    o_ref[...] = x_ref[...] + acc


def _outproj(ya, yb, yb_col, w, x, tm):
    n, d = x.shape
    ka = d // 2
    tm = min(tm, n)
    return pl.pallas_call(
        _outproj_kernel,
        out_shape=jax.ShapeDtypeStruct((n, d), F32),
        grid=(n // tm,),
        in_specs=[pl.BlockSpec((tm, ka), lambda i: (i, 0)),
                  pl.BlockSpec((tm, ka), lambda i: (i, yb_col)),
                  pl.BlockSpec((d, d), lambda i: (0, 0)),
                  pl.BlockSpec((tm, d), lambda i: (i, 0))],
        out_specs=pl.BlockSpec((tm, d), lambda i: (i, 0)),
        compiler_params=_cparams("parallel"),
        name="outproj",
    )(ya, yb, w, x)


def _conv_kernel(b_ref, c_ref, u_ref, hc_ref, hu_ref, w_ref, o_ref, *, tiles_per_seq):
    tm = c_ref.shape[0]
    first = (pl.program_id(0) % tiles_per_seq) == 0
    v = c_ref[...] * u_ref[...]
    halo = jnp.where(first, 0.0, hc_ref[...] * hu_ref[...])
    row = lax.broadcasted_iota(jnp.int32, v.shape, 0)
    v1 = jnp.where(row == 0, halo[7:8, :], pltpu.roll(v, 1, axis=0))
    v2 = jnp.where(row == 0, halo[6:7, :],
                   jnp.where(row == 1, halo[7:8, :], pltpu.roll(v, 2, axis=0)))
    w = w_ref[...]
    conv = w[2:3, :] * v + w[1:2, :] * v1 + w[0:1, :] * v2
    o_ref[...] = (b_ref[...] * conv).astype(o_ref.dtype)


def _gated_conv(proj, conv_w, seq, tm):
    n = proj.shape[0]
    cw = conv_w.shape[1]
    tm = min(tm, seq)
    hb = tm // 8
    return pl.pallas_call(
        functools.partial(_conv_kernel, tiles_per_seq=seq // tm),
        out_shape=jax.ShapeDtypeStruct((n, cw), BF16),
        grid=(n // tm,),
        in_specs=[pl.BlockSpec((tm, cw), lambda i: (i, 0)),
                  pl.BlockSpec((tm, cw), lambda i: (i, 1)),
                  pl.BlockSpec((tm, cw), lambda i: (i, 2)),
                  pl.BlockSpec((8, cw), lambda i: (jnp.maximum(i * hb - 1, 0), 1)),
                  pl.BlockSpec((8, cw), lambda i: (jnp.maximum(i * hb - 1, 0), 2)),
                  pl.BlockSpec((conv_w.shape[0], cw), lambda i: (0, 0))],
        out_specs=pl.BlockSpec((tm, cw), lambda i: (i, 0)),
        compiler_params=_cparams("parallel"),
        name="gated_conv",
    )(proj, proj, proj, proj, proj, conv_w)


def _rope_lanes(a, cos_t, sin_t):
    return a * cos_t + pltpu.roll(a, LANES // 2, axis=1) * sin_t


def _mla_prep_kernel(cq_ref, ckv_ref, kr_ref, gq_ref, gkv_ref, wq_ref, wkv_ref, cos_ref, sin_ref,
                     q_ref, k_ref, v_ref, *, heads):
    cqn = _rms(cq_ref[...], gq_ref[...]).astype(BF16)
    ckvn = _rms(ckv_ref[...], gkv_ref[...]).astype(BF16)
    cos_t, sin_t = cos_ref[...], sin_ref[...]
    q = jnp.dot(cqn, wq_ref[...], preferred_element_type=F32)
    kv = jnp.dot(ckvn, wkv_ref[...], preferred_element_type=F32)
    k_rope = _rope_lanes(kr_ref[...], cos_t, sin_t).astype(BF16)
    for h in range(heads):
        c0 = h * 2 * LANES
        q_ref[h, :, :LANES] = q[:, c0:c0 + LANES].astype(BF16)
        q_ref[h, :, LANES:] = _rope_lanes(q[:, c0 + LANES:c0 + 2 * LANES], cos_t, sin_t).astype(BF16)
        k_ref[h, :, :LANES] = kv[:, c0:c0 + LANES].astype(BF16)
        k_ref[h, :, LANES:] = k_rope
        v_ref[h] = kv[:, c0 + LANES:c0 + 2 * LANES].astype(BF16)


def _mla_prep(proj, col0, g_q, g_kv, wq, wkv, cos_t, sin_t, heads, seq, tm):
    n = proj.shape[0]
    qr, kvr = g_q.shape[0], g_kv.shape[0]
    tm = min(tm, seq)
    ps = seq // tm
    assert col0 % qr == 0 and (col0 + qr) % kvr == 0 and (col0 + qr + kvr) % LANES == 0
    return pl.pallas_call(
        functools.partial(_mla_prep_kernel, heads=heads),
        out_shape=(jax.ShapeDtypeStruct((heads, n, 2 * LANES), BF16),
                   jax.ShapeDtypeStruct((heads, n, 2 * LANES), BF16),
                   jax.ShapeDtypeStruct((heads, n, LANES), BF16)),
        grid=(n // tm,),
        in_specs=[pl.BlockSpec((tm, qr), lambda i: (i, col0 // qr)),
                  pl.BlockSpec((tm, kvr), lambda i: (i, (col0 + qr) // kvr)),
                  pl.BlockSpec((tm, LANES), lambda i: (i, (col0 + qr + kvr) // LANES)),
                  pl.BlockSpec((1, qr), lambda i: (0, 0)),
                  pl.BlockSpec((1, kvr), lambda i: (0, 0)),
                  pl.BlockSpec(wq.shape, lambda i: (0, 0)),
                  pl.BlockSpec(wkv.shape, lambda i: (0, 0)),
                  pl.BlockSpec((tm, LANES), lambda i: (i % ps, 0)),
                  pl.BlockSpec((tm, LANES), lambda i: (i % ps, 0))],
        out_specs=(pl.BlockSpec((heads, tm, 2 * LANES), lambda i: (0, i, 0)),
                   pl.BlockSpec((heads, tm, 2 * LANES), lambda i: (0, i, 0)),
                   pl.BlockSpec((heads, tm, LANES), lambda i: (0, i, 0))),
        compiler_params=_cparams("parallel"),
        name="mla_prep",
    )(proj, proj, proj, g_q.reshape(1, qr), g_kv.reshape(1, kvr), wq, wkv, cos_t, sin_t)


def _online_step(s, vblk, carry):
    m, l, acc = carry
    m_new = jnp.maximum(m, jnp.max(s, axis=1, keepdims=True))
    alpha = jnp.exp(m - m_new)
    p = jnp.exp(s - m_new)
    l = alpha * l + jnp.sum(p, axis=1, keepdims=True)
    acc = alpha * acc + jnp.dot(p.astype(BF16), vblk, preferred_element_type=F32)
    return m_new, l, acc


def _qk(q, k):
    return lax.dot_general(q, k, (((1,), (1,)), ((), ())), preferred_element_type=F32)


def _softmax_init(tq, dv):
    return (jnp.full((tq, 1), -1e30, F32), jnp.zeros((tq, 1), F32), jnp.zeros((tq, dv), F32))


def _mla_flash_kernel(q_ref, k_ref, v_ref, o_ref):
    tq = q_ref.shape[1]
    qi = pl.program_id(2)
    q = q_ref[0]

    def body(j, carry):
        off = pl.multiple_of(j * tq, tq)
        return _online_step(_qk(q, k_ref[0, pl.ds(off, tq), :]), v_ref[0, pl.ds(off, tq), :], carry)

    carry = lax.fori_loop(0, qi, body, _softmax_init(tq, v_ref.shape[2]))
    off = pl.multiple_of(qi * tq, tq)
    causal = (lax.broadcasted_iota(jnp.int32, (tq, tq), 1) <= lax.broadcasted_iota(jnp.int32, (tq, tq), 0))
    s = jnp.where(causal, _qk(q, k_ref[0, pl.ds(off, tq), :]), -jnp.inf)
    _, l, acc = _online_step(s, v_ref[0, pl.ds(off, tq), :], carry)
    o_ref[...] = (acc / l).astype(o_ref.dtype)


def _mla_flash(q, k, v, batch, seq, tq):
    heads, n, dk = q.shape
    dv = v.shape[2]
    tq = min(tq, seq)
    nq = seq // tq
    return pl.pallas_call(
        _mla_flash_kernel,
        out_shape=jax.ShapeDtypeStruct((n, heads * dv), BF16),
        grid=(batch, heads, nq),
        in_specs=[pl.BlockSpec((1, tq, dk), lambda b, h, i: (h, b * nq + i, 0)),
                  pl.BlockSpec((1, seq, dk), lambda b, h, i: (h, b, 0)),
                  pl.BlockSpec((1, seq, dv), lambda b, h, i: (h, b, 0))],
        out_specs=pl.BlockSpec((tq, dv), lambda b, h, i: (b * nq + i, h)),
        compiler_params=_cparams("parallel", "parallel", "arbitrary"),
        name="mla_flash",
    )(q, k, v)


def _diff_flash_kernel(lam_ref, slopes_ref, q_ref, k_ref, v_ref, g_ref, o_ref, *, out_scale):
    tq = q_ref.shape[0]
    dh = q_ref.shape[1] // 2
    dv = v_ref.shape[1]
    qi = pl.program_id(2)
    slope = slopes_ref[0, pl.program_id(1)]
    q1, q2 = q_ref[:, :dh], q_ref[:, dh:]
    col = lax.broadcasted_iota(jnp.int32, (1, tq), 1).astype(F32)

    def scores(off, qpart, lo):
        return _qk(qpart, k_ref[pl.ds(off, tq), lo:lo + dh])

    def body(j, carry):
        c1, c2 = carry
        off = pl.multiple_of(j * tq, tq)
        bias = slope * (col + (off - qi * tq).astype(F32))
        vblk = v_ref[pl.ds(off, tq), :]
        c1 = _online_step(scores(off, q1, 0) + bias, vblk, c1)
        c2 = _online_step(scores(off, q2, dh) + bias, vblk, c2)
        return c1, c2

    c1, c2 = lax.fori_loop(0, qi, body, (_softmax_init(tq, dv), _softmax_init(tq, dv)))
    off = pl.multiple_of(qi * tq, tq)
    causal = (lax.broadcasted_iota(jnp.int32, (tq, tq), 1) <= lax.broadcasted_iota(jnp.int32, (tq, tq), 0))
    bias = slope * col
    vblk = v_ref[pl.ds(off, tq), :]
    _, l1, a1 = _online_step(jnp.where(causal, scores(off, q1, 0) + bias, -jnp.inf), vblk, c1)
    _, l2, a2 = _online_step(jnp.where(causal, scores(off, q2, dh) + bias, -jnp.inf), vblk, c2)
    o = a1 / l1 - lam_ref[0, 0] * (a2 / l2)
    o_ref[...] = (_rms(o, g_ref[...]) * out_scale).astype(o_ref.dtype)


def _diff_flash(qkv, lam, g_subln, batch, seq, heads, out_scale, tq):
    n = qkv.shape[0]
    dv = g_subln.shape[0]
    tq = min(tq, seq)
    nq = seq // tq
    slopes = jnp.asarray((2.0 ** (-8.0 * np.arange(1, heads + 1) / heads)).astype(np.float32)).reshape(1, heads)
    return pl.pallas_call(
        functools.partial(_diff_flash_kernel, out_scale=out_scale),
        out_shape=jax.ShapeDtypeStruct((n, heads * dv), BF16),
        grid=(batch, heads, nq),
        in_specs=[pl.BlockSpec(memory_space=pltpu.SMEM),
                  pl.BlockSpec(memory_space=pltpu.SMEM),
                  pl.BlockSpec((tq, dv), lambda b, h, i: (b * nq + i, h)),
                  pl.BlockSpec((seq, dv), lambda b, h, i: (b, heads + h)),
                  pl.BlockSpec((seq, dv), lambda b, h, i: (b, 2 * heads + h)),
                  pl.BlockSpec((1, dv), lambda b, h, i: (0, 0))],
        out_specs=pl.BlockSpec((tq, dv), lambda b, h, i: (b * nq + i, h)),
        compiler_params=_cparams("parallel", "parallel", "arbitrary"),
        name="diff_flash",
    )(lam.reshape(1, 1), slopes, qkv, qkv, qkv, g_subln.reshape(1, dv))


def _router_kernel(x_ref, g_ref, w_ref, b_ref, h_ref, ids_ref, wts_ref, *, n_groups, n_experts):
    h = _rms(x_ref[...], g_ref[...])
    h_ref[...] = h
    logits = jnp.dot(h, w_ref[...], preferred_element_type=F32, precision=lax.Precision.HIGHEST) + b_ref[...]
    lane = lax.broadcasted_iota(jnp.int32, logits.shape, 1).astype(F32)
    neg = -jnp.inf
    per_group = n_experts // n_groups
    gl = jnp.where(lane < n_groups, logits, neg)
    gmax = jnp.max(gl, axis=1, keepdims=True)
    g_prob = 1.0 / jnp.sum(jnp.exp(gl - gmax), axis=1, keepdims=True)
    g_idx = jnp.min(jnp.where(gl == gmax, lane, float(LANES)), axis=1, keepdims=True)
    lo = n_groups + g_idx * per_group
    el = jnp.where((lane >= lo) & (lane < lo + per_group), logits, neg)
    m1 = jnp.max(el, axis=1, keepdims=True)
    i1 = jnp.min(jnp.where(el == m1, lane, float(LANES)), axis=1, keepdims=True)
    el2 = jnp.where(lane == i1, neg, el)
    m2 = jnp.max(el2, axis=1, keepdims=True)
    i2 = jnp.min(jnp.where(el2 == m2, lane, float(LANES)), axis=1, keepdims=True)
    e2 = jnp.exp(m2 - m1)
    w1 = g_prob / (1.0 + e2)
    w2 = g_prob * e2 / (1.0 + e2)
    ids = jnp.where(lane == 0.0, i1 - n_groups, jnp.where(lane == 1.0, i2 - n_groups, 0.0))
    ids_ref[...] = ids.astype(jnp.int32)
    wts_ref[...] = jnp.where(lane == 0.0, w1, jnp.where(lane == 1.0, w2, 0.0))


def _router(x, g, w_r, b_r, n_groups, n_experts, tm):
    n, d = x.shape
    tm = min(tm, n)
    return pl.pallas_call(
        functools.partial(_router_kernel, n_groups=n_groups, n_experts=n_experts),
        out_shape=(jax.ShapeDtypeStruct((n, d), F32),
                   jax.ShapeDtypeStruct((n, LANES), jnp.int32),
                   jax.ShapeDtypeStruct((n, LANES), F32)),
        grid=(n // tm,),
        in_specs=[pl.BlockSpec((tm, d), lambda i: (i, 0)),
                  pl.BlockSpec((1, d), lambda i: (0, 0)),
                  pl.BlockSpec((d, LANES), lambda i: (0, 0)),
                  pl.BlockSpec((1, LANES), lambda i: (0, 0))],
        out_specs=(pl.BlockSpec((tm, d), lambda i: (i, 0)),
                   pl.BlockSpec((tm, LANES), lambda i: (i, 0)),
                   pl.BlockSpec((tm, LANES), lambda i: (i, 0))),
        compiler_params=_cparams("parallel"),
        name="router",
    )(x, g.reshape(1, d), w_r, b_r)


def _moe_plan(ids, n_experts, tile):
    n = ids.shape[0]
    na = TOP_K * n
    n_tiles = na // tile + n_experts
    ef = ids[:, :TOP_K].T.reshape(na)
    onehot = (ef[:, None] == jnp.arange(n_experts, dtype=jnp.int32)[None, :]).astype(jnp.int32)
    csum = jnp.cumsum(onehot, axis=0)
    rank = jnp.sum(csum * onehot, axis=1) - 1
    counts = csum[-1]
    tiles_per = (counts + tile - 1) // tile
    tile_end = jnp.cumsum(tiles_per)
    tile_start = tile_end - tiles_per
    pos = tile_start[ef] * tile + rank
    a = jnp.arange(na, dtype=jnp.int32)
    src_tok = jnp.zeros((n_tiles * tile,), jnp.int32).at[pos].set(a % n)
    dst_row = jnp.zeros((n_tiles * tile,), jnp.int32).at[pos].set(a)
    t = jnp.arange(n_tiles, dtype=jnp.int32)
    used = tile_end[-1]
    tile_expert = jnp.sum((tile_end[None, :] <= jnp.minimum(t, used - 1)[:, None]).astype(jnp.int32), axis=1)
    rows = jnp.clip(counts[tile_expert] - (t - tile_start[tile_expert]) * tile, 0, tile)
    rows = jnp.where(t < used, rows, 0)
    rows = jnp.concatenate([rows, jnp.zeros((1,), rows.dtype)]).astype(jnp.int32)
    return tile_expert, rows, src_tok, dst_row


def _expert_kernel(te_ref, rows_ref, src_ref, dst_ref, h_hbm, wug_ref, wd_ref, out_hbm,
                   xbuf, ybuf, gsem, ssem, *, tile, ff):
    i = pl.program_id(0)
    last = pl.num_programs(0) - 1
    slot = i % 2

    def row_copy(t, s, r, gather):
        if gather:
            return pltpu.make_async_copy(h_hbm.at[pl.ds(src_ref[t * tile + r], 1)],
                                         xbuf.at[s, pl.ds(r, 1)], gsem.at[s])
        return pltpu.make_async_copy(ybuf.at[s, pl.ds(r, 1)],
                                     out_hbm.at[pl.ds(dst_ref[t * tile + r], 1)], ssem.at[s])

    def start_rows(t, s, gather):
        nrows = rows_ref[t]

        def issue(g, c):
            for u in range(DMA_UNROLL):
                r = g * DMA_UNROLL + u

                @pl.when(r < nrows)
                def _():
                    row_copy(t, s, r, gather).start()
            return c

        lax.fori_loop(0, tile // DMA_UNROLL, issue, 0)

    def wait_rows(t, s, gather):
        nrows = rows_ref[t]
        whole = pl.multiple_of((nrows // SUBLANES) * SUBLANES, SUBLANES)

        def span(r0, size):
            if gather:
                return pltpu.make_async_copy(h_hbm.at[pl.ds(r0, size)], xbuf.at[s, pl.ds(r0, size)], gsem.at[s])
            return pltpu.make_async_copy(ybuf.at[s, pl.ds(r0, size)], out_hbm.at[pl.ds(r0, size)], ssem.at[s])

        @pl.when(whole > 0)
        def _():
            span(0, whole).wait()

        for u in range(SUBLANES - 1):
            @pl.when(u < nrows - whole)
            def _():
                span(u, 1).wait()

    @pl.when(i == 0)
    def _():
        xbuf[...] = jnp.zeros_like(xbuf)

        @pl.when(rows_ref[0] > 0)
        def _():
            start_rows(0, 0, True)

    @pl.when(rows_ref[i + 1] > 0)
    def _():
        start_rows(i + 1, 1 - slot, True)

    @pl.when(i >= 2)
    def _():
        @pl.when(rows_ref[jnp.maximum(i - 2, 0)] > 0)
        def _():
            wait_rows(i - 2, slot, False)

    @pl.when(rows_ref[i] > 0)
    def _():
        wait_rows(i, slot, True)
        x = xbuf[slot].astype(BF16)
        gu = jnp.dot(x, wug_ref[0], preferred_element_type=F32)
        gate, up = gu[:, :ff], gu[:, ff:]
        act = (gate * jax.nn.sigmoid(gate) * up).astype(BF16)
        ybuf[slot] = jnp.dot(act, wd_ref[0], preferred_element_type=F32)
        start_rows(i, slot, False)

    @pl.when(i == last)
    def _():
        @pl.when(rows_ref[jnp.maximum(i - 1, 0)] > 0)
        def _():
            wait_rows(i - 1, 1 - slot, False)

        @pl.when(rows_ref[i] > 0)
        def _():
            wait_rows(i, slot, False)


def _experts(h, plan, wug, wd, tile):
    tile_expert, rows, src_tok, dst_row = plan
    n, d = h.shape
    ff = wd.shape[1]
    n_tiles = tile_expert.shape[0]
    assert n_tiles >= 2
    grid_spec = pltpu.PrefetchScalarGridSpec(
        num_scalar_prefetch=4,
        grid=(n_tiles,),
        in_specs=[pl.BlockSpec(memory_space=pl.ANY),
                  pl.BlockSpec((1, d, 2 * ff), lambda i, te, u, s, t: (te[i], 0, 0)),
                  pl.BlockSpec((1, ff, d), lambda i, te, u, s, t: (te[i], 0, 0))],
        out_specs=pl.BlockSpec(memory_space=pl.ANY),
        scratch_shapes=[pltpu.VMEM((2, tile, d), F32), pltpu.VMEM((2, tile, d), F32),
                        pltpu.SemaphoreType.DMA((2,)), pltpu.SemaphoreType.DMA((2,))],
    )
    return pl.pallas_call(
        functools.partial(_expert_kernel, tile=tile, ff=ff),
        out_shape=jax.ShapeDtypeStruct((TOP_K * n, d), F32),
        grid_spec=grid_spec,
        compiler_params=_cparams("arbitrary"),
        name="experts",
    )(tile_expert, rows, src_tok, dst_row, h, wug, wd)


def _combine_kernel(x_ref, y0_ref, y1_ref, wts_ref, g_ref, *out_refs, emit_x):
    wts = wts_ref[...]
    x_new = x_ref[...] + wts[:, 0:1] * y0_ref[...] + wts[:, 1:2] * y1_ref[...]
    normed = _rms(x_new, g_ref[...])
    if emit_x:
        out_refs[0][...] = x_new
        out_refs[1][...] = normed.astype(out_refs[1].dtype)
    else:
        out_refs[0][...] = normed.astype(out_refs[0].dtype)


def _combine(x, y2, wts, g, emit_x, norm_dtype, tm):
    n, d = x.shape
    tm = min(tm, n)
    nb = n // tm
    row = pl.BlockSpec((tm, d), lambda i: (i, 0))
    out_shape = [jax.ShapeDtypeStruct((n, d), norm_dtype)]
    out_specs = [row]
    if emit_x:
        out_shape.insert(0, jax.ShapeDtypeStruct((n, d), F32))
        out_specs.insert(0, row)
    return pl.pallas_call(
        functools.partial(_combine_kernel, emit_x=emit_x),
        out_shape=tuple(out_shape),
        grid=(nb,),
        in_specs=[row, row, pl.BlockSpec((tm, d), lambda i: (nb + i, 0)),
                  pl.BlockSpec((tm, LANES), lambda i: (i, 0)),
                  pl.BlockSpec((1, d), lambda i: (0, 0))],
        out_specs=tuple(out_specs),
        compiler_params=_cparams("parallel"),
        name="combine",
    )(x, y2, y2, wts, g.reshape(1, d))


def _hier_moe(x_mid, g_ffn, w_group, b_group, w_expert, b_expert, w_up_gate, w_down, g_next, emit_x, norm_dtype):
    n, d = x_mid.shape
    n_experts = w_expert.shape[1]
    pad = LANES - N_GROUPS - n_experts
    w_r = jnp.concatenate([w_group, w_expert, jnp.zeros((d, pad), F32)], axis=1)
    b_r = jnp.concatenate([b_group, b_expert, jnp.zeros((pad,), F32)]).reshape(1, LANES)
    h, ids, wts = _router(x_mid, g_ffn, w_r, b_r, N_GROUPS, n_experts, tm=256)
    plan = _moe_plan(ids, n_experts, MOE_TILE)
    y2 = _experts(h, plan, w_up_gate.astype(BF16), w_down.astype(BF16), MOE_TILE)
    return _combine(x_mid, y2, wts, g_next, emit_x, norm_dtype, tm=256)


def _rope_tables(seq, rope_dim):
    pos = jnp.arange(seq, dtype=F32)
    inv_freq = ROPE_THETA ** (-jnp.arange(0, rope_dim, 2, dtype=F32) / rope_dim)
    ang = pos[:, None] * inv_freq[None, :]
    cos, sin = jnp.cos(ang), jnp.sin(ang)
    zero = jnp.zeros_like(cos)
    return (jnp.concatenate([cos, cos, zero, zero], axis=1),
            jnp.concatenate([-sin, sin, zero, zero], axis=1))


def kernel(x, l0_g_mix, l0_w_in, l0_conv_w, l0_g_q_lat, l0_w_uq, l0_g_kv_lat, l0_w_ukv, l0_w_out, l0_g_ffn, l0_w_group, l0_b_group, l0_w_expert, l0_b_expert, l0_w_up_gate, l0_w_down, l1_g_mix, l1_w_qkv, l1_lam_q1, l1_lam_k1, l1_lam_q2, l1_lam_k2, l1_g_subln, l1_w_out, l1_g_ffn, l1_w_group, l1_b_group, l1_w_expert, l1_b_expert, l1_w_up_gate, l1_w_down, g_final):
    batch, seq, d = x.shape
    n = batch * seq
    x2 = x.reshape(n, d)
    cw = l0_conv_w.shape[1]
    qr, kvr = l0_g_q_lat.shape[0], l0_g_kv_lat.shape[0]
    heads = (d - cw) // LANES
    rope = l0_w_in.shape[1] - 3 * cw - qr - kvr
    nope = l0_w_uq.shape[1] // heads - rope
    assert rope == LANES // 2 and nope == LANES

    half = rope // 2
    kr0 = 3 * cw + qr + kvr
    t1, t2 = l0_w_in[:, kr0:kr0 + half], l0_w_in[:, kr0 + half:kr0 + rope]
    used_cols = kr0 + 2 * rope
    in_width = -(-used_cols // 512) * 512
    w_in = jnp.concatenate([l0_w_in[:, :kr0], t1, t2, t2, t1, jnp.zeros((d, in_width - used_cols), F32)], axis=1)
    proj = _norm_matmul(x2, l0_g_mix, w_in.astype(BF16), F32, tm=1024, tn=512)
    y_conv = _gated_conv(proj, l0_conv_w, seq, tm=512)

    scale_mla = (nope + rope) ** -0.5
    wq = l0_w_uq.reshape(qr, heads, nope + rope) * scale_mla
    wq = jnp.concatenate([wq[..., :nope], wq[..., nope:nope + half], wq[..., nope + half:],
                          wq[..., nope + half:], wq[..., nope:nope + half]], axis=-1)
    wq = wq.reshape(qr, heads * 2 * LANES).astype(BF16)
    cos_t, sin_t = _rope_tables(seq, rope)
    q, k, v = _mla_prep(proj, 3 * cw, l0_g_q_lat, l0_g_kv_lat, wq, l0_w_ukv.astype(BF16), cos_t, sin_t,
                        heads, seq, tm=256)
    y_mla = _mla_flash(q, k, v, batch, seq, tq=512)
    x_mid = _outproj(y_conv, y_mla, 0, l0_w_out.astype(BF16), x2, tm=512)
    x1, h1 = _hier_moe(x_mid, l0_g_ffn, l0_w_group, l0_b_group, l0_w_expert, l0_b_expert,
                       l0_w_up_gate, l0_w_down, l1_g_mix, True, BF16)

    dv = l1_g_subln.shape[0]
    dheads = d // dv
    dh = dv // 2
    lam_init = 0.8 - 0.6 * math.exp(-0.3 * 1)
    lam = (jnp.exp(jnp.sum(l1_lam_q1 * l1_lam_k1)) - jnp.exp(jnp.sum(l1_lam_q2 * l1_lam_k2)) + lam_init)
    w_qkv = jnp.concatenate([l1_w_qkv[:, :d] * dh ** -0.5, l1_w_qkv[:, d:]], axis=1).astype(BF16)
    qkv = _matmul(h1, w_qkv, BF16, tm=1024, tn=512)
    o = _diff_flash(qkv, lam, l1_g_subln, batch, seq, dheads, 1.0 - lam_init, tq=512)
    x_mid = _outproj(o, o, 1, l1_w_out.astype(BF16), x1, tm=512)
    (out,) = _hier_moe(x_mid, l1_g_ffn, l1_w_group, l1_b_group, l1_w_expert, l1_b_expert,
                       l1_w_up_gate, l1_w_down, g_final, False, F32)
    return out.reshape(batch, seq, d)
```

```python
import functools
import math

import jax
import jax.numpy as jnp
import numpy as np
from jax import lax
from jax.experimental import pallas as pl
from jax.experimental.pallas import tpu as pltpu

F32 = jnp.float32
BF16 = jnp.bfloat16

NORM_EPS = 1e-6
ROPE_THETA = 10000.0
LANES = 128
SUBLANES = 8
MOE_TILE = 256
NORM_ROWS = 256
CAST_ROWS = 256
N_GROUPS = 4
TOP_K = 2
VMEM_LIMIT_BYTES = 52 * 1024 * 1024


def _cparams(*sem):
    return pltpu.CompilerParams(dimension_semantics=sem, vmem_limit_bytes=VMEM_LIMIT_BYTES)


def _rms(xf, g):
    ms = jnp.mean(xf * xf, axis=-1, keepdims=True)
    return xf * lax.rsqrt(ms + NORM_EPS) * g


def _norm_matmul_kernel(x_ref, g_ref, w_ref, o_ref, h_ref):
    @pl.when(pl.program_id(1) == 0)
    def _():
        rows = min(NORM_ROWS, x_ref.shape[0])
        for r0 in range(0, x_ref.shape[0], rows):
            h_ref[r0:r0 + rows, :] = _rms(x_ref[r0:r0 + rows, :], g_ref[...]).astype(BF16)

    o_ref[...] = jnp.dot(h_ref[...], w_ref[...], preferred_element_type=F32).astype(o_ref.dtype)


def _norm_matmul(x, g, w, out_dtype, tm, tn):
    n, k = x.shape
    nn = w.shape[1]
    tm, tn = min(tm, n), min(tn, nn)
    return pl.pallas_call(
        _norm_matmul_kernel,
        out_shape=jax.ShapeDtypeStruct((n, nn), out_dtype),
        grid=(n // tm, nn // tn),
        in_specs=[pl.BlockSpec((tm, k), lambda i, j: (i, 0)),
                  pl.BlockSpec((1, k), lambda i, j: (0, 0)),
                  pl.BlockSpec((k, tn), lambda i, j: (0, j))],
        out_specs=pl.BlockSpec((tm, tn), lambda i, j: (i, j)),
        scratch_shapes=[pltpu.VMEM((tm, k), BF16)],
        compiler_params=_cparams("parallel", "arbitrary"),
        name="norm_matmul",
    )(x, g.reshape(1, k), w)


def _matmul_kernel(x_ref, w_ref, o_ref):
    o_ref[...] = jnp.dot(x_ref[...], w_ref[...], preferred_element_type=F32).astype(o_ref.dtype)


def _matmul(x, w, out_dtype, tm, tn):
    n, k = x.shape
    nn = w.shape[1]
    tm, tn = min(tm, n), min(tn, nn)
    return pl.pallas_call(
        _matmul_kernel,
        out_shape=jax.ShapeDtypeStruct((n, nn), out_dtype),
        grid=(n // tm, nn // tn),
        in_specs=[pl.BlockSpec((tm, k), lambda i, j: (i, 0)),
                  pl.BlockSpec((k, tn), lambda i, j: (0, j))],
        out_specs=pl.BlockSpec((tm, tn), lambda i, j: (i, j)),
        compiler_params=_cparams("parallel", "arbitrary"),
        name="matmul",
    )(x, w)


def _outproj_kernel(ya_ref, yb_ref, w_ref, x_ref, o_ref):
    ka = ya_ref.shape[1]
    acc = jnp.dot(ya_ref[...], w_ref[:ka, :], preferred_element_type=F32)
    acc = acc + jnp.dot(yb_ref[...], w_ref[ka:, :], preferred_element_type=F32)
    o_ref[...] = x_ref[...] + acc


def _outproj(ya, yb, yb_col, w, x, tm):
    n, d = x.shape
    ka = d // 2
    tm = min(tm, n)
    return pl.pallas_call(
        _outproj_kernel,
        out_shape=jax.ShapeDtypeStruct((n, d), F32),
        grid=(n // tm,),
        in_specs=[pl.BlockSpec((tm, ka), lambda i: (i, 0)),
                  pl.BlockSpec((tm, ka), lambda i: (i, yb_col)),
                  pl.BlockSpec((d, d), lambda i: (0, 0)),
                  pl.BlockSpec((tm, d), lambda i: (i, 0))],
        out_specs=pl.BlockSpec((tm, d), lambda i: (i, 0)),
        compiler_params=_cparams("parallel"),
        name="outproj",
    )(ya, yb, w, x)


def _conv_kernel(b_ref, c_ref, u_ref, hc_ref, hu_ref, w_ref, o_ref, *, tiles_per_seq):
    tm = c_ref.shape[0]
    first = (pl.program_id(0) % tiles_per_seq) == 0
    v = c_ref[...] * u_ref[...]
    halo = jnp.where(first, 0.0, hc_ref[...] * hu_ref[...])
    row = lax.broadcasted_iota(jnp.int32, v.shape, 0)
    v1 = jnp.where(row == 0, halo[7:8, :], pltpu.roll(v, 1, axis=0))
    v2 = jnp.where(row == 0, halo[6:7, :],
                   jnp.where(row == 1, halo[7:8, :], pltpu.roll(v, 2, axis=0)))
    w = w_ref[...]
    conv = w[2:3, :] * v + w[1:2, :] * v1 + w[0:1, :] * v2
    o_ref[...] = (b_ref[...] * conv).astype(o_ref.dtype)


def _gated_conv(proj, conv_w, seq, tm):
    n = proj.shape[0]
    cw = conv_w.shape[1]
    tm = min(tm, seq)
    hb = tm // 8
    return pl.pallas_call(
        functools.partial(_conv_kernel, tiles_per_seq=seq // tm),
        out_shape=jax.ShapeDtypeStruct((n, cw), BF16),
        grid=(n // tm,),
        in_specs=[pl.BlockSpec((tm, cw), lambda i: (i, 0)),
                  pl.BlockSpec((tm, cw), lambda i: (i, 1)),
                  pl.BlockSpec((tm, cw), lambda i: (i, 2)),
                  pl.BlockSpec((8, cw), lambda i: (jnp.maximum(i * hb - 1, 0), 1)),
                  pl.BlockSpec((8, cw), lambda i: (jnp.maximum(i * hb - 1, 0), 2)),
                  pl.BlockSpec((conv_w.shape[0], cw), lambda i: (0, 0))],
        out_specs=pl.BlockSpec((tm, cw), lambda i: (i, 0)),
        compiler_params=_cparams("parallel"),
        name="gated_conv",
    )(proj, proj, proj, proj, proj, conv_w)


def _rope_lanes(a, cos_t, sin_t):
    return a * cos_t + pltpu.roll(a, LANES // 2, axis=1) * sin_t


def _mla_prep_kernel(cq_ref, ckv_ref, kr_ref, gq_ref, gkv_ref, wq_ref, wkv_ref, cos_ref, sin_ref,
                     q_ref, k_ref, v_ref, *, heads):
    cqn = _rms(cq_ref[...], gq_ref[...]).astype(BF16)
    ckvn = _rms(ckv_ref[...], gkv_ref[...]).astype(BF16)
    cos_t, sin_t = cos_ref[...], sin_ref[...]
    q = jnp.dot(cqn, wq_ref[...], preferred_element_type=F32)
    kv = jnp.dot(ckvn, wkv_ref[...], preferred_element_type=F32)
    k_rope = _rope_lanes(kr_ref[...], cos_t, sin_t).astype(BF16)
    for h in range(heads):
        c0 = h * 2 * LANES
        q_ref[h, :, :LANES] = q[:, c0:c0 + LANES].astype(BF16)
        q_ref[h, :, LANES:] = _rope_lanes(q[:, c0 + LANES:c0 + 2 * LANES], cos_t, sin_t).astype(BF16)
        k_ref[h, :, :LANES] = kv[:, c0:c0 + LANES].astype(BF16)
        k_ref[h, :, LANES:] = k_rope
        v_ref[h] = kv[:, c0 + LANES:c0 + 2 * LANES].astype(BF16)


def _mla_prep(proj, col0, g_q, g_kv, wq, wkv, cos_t, sin_t, heads, seq, tm):
    n = proj.shape[0]
    qr, kvr = g_q.shape[0], g_kv.shape[0]
    tm = min(tm, seq)
    ps = seq // tm
    assert col0 % qr == 0 and (col0 + qr) % kvr == 0 and (col0 + qr + kvr) % LANES == 0
    return pl.pallas_call(
        functools.partial(_mla_prep_kernel, heads=heads),
        out_shape=(jax.ShapeDtypeStruct((heads, n, 2 * LANES), BF16),
                   jax.ShapeDtypeStruct((heads, n, 2 * LANES), BF16),
                   jax.ShapeDtypeStruct((heads, n, LANES), BF16)),
        grid=(n // tm,),
        in_specs=[pl.BlockSpec((tm, qr), lambda i: (i, col0 // qr)),
                  pl.BlockSpec((tm, kvr), lambda i: (i, (col0 + qr) // kvr)),
                  pl.BlockSpec((tm, LANES), lambda i: (i, (col0 + qr + kvr) // LANES)),
                  pl.BlockSpec((1, qr), lambda i: (0, 0)),
                  pl.BlockSpec((1, kvr), lambda i: (0, 0)),
                  pl.BlockSpec(wq.shape, lambda i: (0, 0)),
                  pl.BlockSpec(wkv.shape, lambda i: (0, 0)),
                  pl.BlockSpec((tm, LANES), lambda i: (i % ps, 0)),
                  pl.BlockSpec((tm, LANES), lambda i: (i % ps, 0))],
        out_specs=(pl.BlockSpec((heads, tm, 2 * LANES), lambda i: (0, i, 0)),
                   pl.BlockSpec((heads, tm, 2 * LANES), lambda i: (0, i, 0)),
                   pl.BlockSpec((heads, tm, LANES), lambda i: (0, i, 0))),
        compiler_params=_cparams("parallel"),
        name="mla_prep",
    )(proj, proj, proj, g_q.reshape(1, qr), g_kv.reshape(1, kvr), wq, wkv, cos_t, sin_t)


def _online_step(s, vblk, carry):
    m, l, acc = carry
    m_new = jnp.maximum(m, jnp.max(s, axis=1, keepdims=True))
    alpha = jnp.exp(m - m_new)
    p = jnp.exp(s - m_new)
    l = alpha * l + jnp.sum(p, axis=1, keepdims=True)
    acc = alpha * acc + jnp.dot(p.astype(BF16), vblk, preferred_element_type=F32)
    return m_new, l, acc


def _qk(q, k):
    return lax.dot_general(q, k, (((1,), (1,)), ((), ())), preferred_element_type=F32)


def _softmax_init(tq, dv):
    return (jnp.full((tq, 1), -1e30, F32), jnp.zeros((tq, 1), F32), jnp.zeros((tq, dv), F32))


def _mla_flash_kernel(q_ref, k_ref, v_ref, o_ref):
    tq = q_ref.shape[1]
    qi = pl.program_id(2)
    q = q_ref[0]

    def body(j, carry):
        off = pl.multiple_of(j * tq, tq)
        return _online_step(_qk(q, k_ref[0, pl.ds(off, tq), :]), v_ref[0, pl.ds(off, tq), :], carry)

    carry = lax.fori_loop(0, qi, body, _softmax_init(tq, v_ref.shape[2]))
    off = pl.multiple_of(qi * tq, tq)
    causal = (lax.broadcasted_iota(jnp.int32, (tq, tq), 1) <= lax.broadcasted_iota(jnp.int32, (tq, tq), 0))
    s = jnp.where(causal, _qk(q, k_ref[0, pl.ds(off, tq), :]), -jnp.inf)
    _, l, acc = _online_step(s, v_ref[0, pl.ds(off, tq), :], carry)
    o_ref[...] = (acc / l).astype(o_ref.dtype)


def _mla_flash(q, k, v, batch, seq, tq):
    heads, n, dk = q.shape
    dv = v.shape[2]
    tq = min(tq, seq)
    nq = seq // tq
    return pl.pallas_call(
        _mla_flash_kernel,
        out_shape=jax.ShapeDtypeStruct((n, heads * dv), BF16),
        grid=(batch, heads, nq),
        in_specs=[pl.BlockSpec((1, tq, dk), lambda b, h, i: (h, b * nq + i, 0)),
                  pl.BlockSpec((1, seq, dk), lambda b, h, i: (h, b, 0)),
                  pl.BlockSpec((1, seq, dv), lambda b, h, i: (h, b, 0))],
        out_specs=pl.BlockSpec((tq, dv), lambda b, h, i: (b * nq + i, h)),
        compiler_params=_cparams("parallel", "parallel", "arbitrary"),
        name="mla_flash",
    )(q, k, v)


def _diff_flash_kernel(lam_ref, slopes_ref, q_ref, k_ref, v_ref, g_ref, o_ref, *, out_scale):
    tq = q_ref.shape[0]
    dh = q_ref.shape[1] // 2
    dv = v_ref.shape[1]
    qi = pl.program_id(2)
    slope = slopes_ref[0, pl.program_id(1)]
    q1, q2 = q_ref[:, :dh], q_ref[:, dh:]
    col = lax.broadcasted_iota(jnp.int32, (1, tq), 1).astype(F32)

    def scores(off, qpart, lo):
        return _qk(qpart, k_ref[pl.ds(off, tq), lo:lo + dh])

    def body(j, carry):
        c1, c2 = carry
        off = pl.multiple_of(j * tq, tq)
        bias = slope * (col + (off - qi * tq).astype(F32))
        vblk = v_ref[pl.ds(off, tq), :]
        c1 = _online_step(scores(off, q1, 0) + bias, vblk, c1)
        c2 = _online_step(scores(off, q2, dh) + bias, vblk, c2)
        return c1, c2

    c1, c2 = lax.fori_loop(0, qi, body, (_softmax_init(tq, dv), _softmax_init(tq, dv)))
    off = pl.multiple_of(qi * tq, tq)
    causal = (lax.broadcasted_iota(jnp.int32, (tq, tq), 1) <= lax.broadcasted_iota(jnp.int32, (tq, tq), 0))
    bias = slope * col
    vblk = v_ref[pl.ds(off, tq), :]
    _, l1, a1 = _online_step(jnp.where(causal, scores(off, q1, 0) + bias, -jnp.inf), vblk, c1)
    _, l2, a2 = _online_step(jnp.where(causal, scores(off, q2, dh) + bias, -jnp.inf), vblk, c2)
    o = a1 / l1 - lam_ref[0, 0] * (a2 / l2)
    o_ref[...] = (_rms(o, g_ref[...]) * out_scale).astype(o_ref.dtype)


def _diff_flash(qkv, lam, g_subln, batch, seq, heads, out_scale, tq):
    n = qkv.shape[0]
    dv = g_subln.shape[0]
    tq = min(tq, seq)
    nq = seq // tq
    slopes = jnp.asarray((2.0 ** (-8.0 * np.arange(1, heads + 1) / heads)).astype(np.float32)).reshape(1, heads)
    return pl.pallas_call(
        functools.partial(_diff_flash_kernel, out_scale=out_scale),
        out_shape=jax.ShapeDtypeStruct((n, heads * dv), BF16),
        grid=(batch, heads, nq),
        in_specs=[pl.BlockSpec(memory_space=pltpu.SMEM),
                  pl.BlockSpec(memory_space=pltpu.SMEM),
                  pl.BlockSpec((tq, dv), lambda b, h, i: (b * nq + i, h)),
                  pl.BlockSpec((seq, dv), lambda b, h, i: (b, heads + h)),
                  pl.BlockSpec((seq, dv), lambda b, h, i: (b, 2 * heads + h)),
                  pl.BlockSpec((1, dv), lambda b, h, i: (0, 0))],
        out_specs=pl.BlockSpec((tq, dv), lambda b, h, i: (b * nq + i, h)),
        compiler_params=_cparams("parallel", "parallel", "arbitrary"),
        name="diff_flash",
    )(lam.reshape(1, 1), slopes, qkv, qkv, qkv, g_subln.reshape(1, dv))


def _router_kernel(x_ref, g_ref, w_ref, b_ref, h_ref, ids_ref, wts_ref, *, n_groups, n_experts):
    h = _rms(x_ref[...], g_ref[...])
    h_ref[...] = h
    logits = jnp.dot(h, w_ref[...], preferred_element_type=F32, precision=lax.Precision.HIGHEST) + b_ref[...]
    lane = lax.broadcasted_iota(jnp.int32, logits.shape, 1).astype(F32)
    neg = -jnp.inf
    per_group = n_experts // n_groups
    gl = jnp.where(lane < n_groups, logits, neg)
    gmax = jnp.max(gl, axis=1, keepdims=True)
    g_prob = 1.0 / jnp.sum(jnp.exp(gl - gmax), axis=1, keepdims=True)
    g_idx = jnp.min(jnp.where(gl == gmax, lane, float(LANES)), axis=1, keepdims=True)
    lo = n_groups + g_idx * per_group
    el = jnp.where((lane >= lo) & (lane < lo + per_group), logits, neg)
    m1 = jnp.max(el, axis=1, keepdims=True)
    i1 = jnp.min(jnp.where(el == m1, lane, float(LANES)), axis=1, keepdims=True)
    el2 = jnp.where(lane == i1, neg, el)
    m2 = jnp.max(el2, axis=1, keepdims=True)
    i2 = jnp.min(jnp.where(el2 == m2, lane, float(LANES)), axis=1, keepdims=True)
    e2 = jnp.exp(m2 - m1)
    w1 = g_prob / (1.0 + e2)
    w2 = g_prob * e2 / (1.0 + e2)
    ids = jnp.where(lane == 0.0, i1 - n_groups, jnp.where(lane == 1.0, i2 - n_groups, 0.0))
    ids_ref[...] = ids.astype(jnp.int32)
    wts_ref[...] = jnp.where(lane == 0.0, w1, jnp.where(lane == 1.0, w2, 0.0))


def _router(x, g, w_r, b_r, n_groups, n_experts, tm):
    n, d = x.shape
    tm = min(tm, n)
    return pl.pallas_call(
        functools.partial(_router_kernel, n_groups=n_groups, n_experts=n_experts),
        out_shape=(jax.ShapeDtypeStruct((n, d), F32),
                   jax.ShapeDtypeStruct((n, LANES), jnp.int32),
                   jax.ShapeDtypeStruct((n, LANES), F32)),
        grid=(n // tm,),
        in_specs=[pl.BlockSpec((tm, d), lambda i: (i, 0)),
                  pl.BlockSpec((1, d), lambda i: (0, 0)),
                  pl.BlockSpec((d, LANES), lambda i: (0, 0)),
                  pl.BlockSpec((1, LANES), lambda i: (0, 0))],
        out_specs=(pl.BlockSpec((tm, d), lambda i: (i, 0)),
                   pl.BlockSpec((tm, LANES), lambda i: (i, 0)),
                   pl.BlockSpec((tm, LANES), lambda i: (i, 0))),
        compiler_params=_cparams("parallel"),
        name="router",
    )(x, g.reshape(1, d), w_r, b_r)


def _moe_plan(ids, n_experts, tile):
    n = ids.shape[0]
    na = TOP_K * n
    n_tiles = na // tile + n_experts
    ef = ids[:, :TOP_K].T.reshape(na)
    onehot = (ef[:, None] == jnp.arange(n_experts, dtype=jnp.int32)[None, :]).astype(jnp.int32)
    csum = jnp.cumsum(onehot, axis=0)
    rank = jnp.sum(csum * onehot, axis=1) - 1
    counts = csum[-1]
    tiles_per = (counts + tile - 1) // tile
    tile_end = jnp.cumsum(tiles_per)
    tile_start = tile_end - tiles_per
    pos = (tile_start[ef] * tile + rank).astype(jnp.int32)
    a = jnp.arange(na, dtype=jnp.int32)
    src_tok = jnp.zeros((n_tiles * tile,), jnp.int32).at[pos].set(a % n)
    t = jnp.arange(n_tiles, dtype=jnp.int32)
    used = tile_end[-1]
    tile_expert = jnp.sum((tile_end[None, :] <= jnp.minimum(t, used - 1)[:, None]).astype(jnp.int32), axis=1)
    live = (t < used).astype(jnp.int32)
    return tile_expert, live, src_tok, pos


def _expert_kernel(te_ref, live_ref, src_ref, h_hbm, wug_ref, wd_ref, o_ref, xbuf, wug_bf, wd_bf, gsem, *, tile, ff):
    i = pl.program_id(0)
    slot = i % 2
    prev = jnp.maximum(i - 1, 0)

    def start_gather(t, s):
        for static_s in range(2):
            @pl.when(s == static_s)
            def _():
                for r in range(tile):
                    pltpu.make_async_copy(h_hbm.at[pl.ds(src_ref[t * tile + r], 1)],
                                          xbuf.at[static_s, pl.ds(r, 1)], gsem.at[static_s]).start()

    def wait_gather(s):
        pltpu.make_async_copy(h_hbm.at[pl.ds(0, tile)], xbuf.at[s], gsem.at[s]).wait()

    @pl.when(i == 0)
    def _():
        start_gather(0, 0)

    @pl.when(live_ref[i] > 0)
    def _():
        @pl.when((i == 0) | (te_ref[i] != te_ref[prev]))
        def _():
            for r0 in range(0, wug_ref.shape[1], CAST_ROWS):
                wug_bf[r0:r0 + CAST_ROWS, :] = wug_ref[0, r0:r0 + CAST_ROWS, :].astype(BF16)
            for r0 in range(0, wd_ref.shape[1], CAST_ROWS):
                wd_bf[r0:r0 + CAST_ROWS, :] = wd_ref[0, r0:r0 + CAST_ROWS, :].astype(BF16)

        wait_gather(slot)
        start_gather(i + 1, 1 - slot)
        gu = jnp.dot(xbuf[slot].astype(BF16), wug_bf[...], preferred_element_type=F32)
        gate, up = gu[:, :ff], gu[:, ff:]
        act = (gate * jax.nn.sigmoid(gate) * up).astype(BF16)
        o_ref[...] = jnp.dot(act, wd_bf[...], preferred_element_type=F32)

    @pl.when(live_ref[i] == 0)
    def _():
        @pl.when((i > 0) & (live_ref[prev] > 0))
        def _():
            wait_gather(slot)
        o_ref[...] = jnp.zeros_like(o_ref)


def _experts(h, plan, wug, wd, tile):
    tile_expert, live, src_tok, _ = plan
    d, ff = wd.shape[2], wd.shape[1]
    n_tiles = tile_expert.shape[0]
    grid_spec = pltpu.PrefetchScalarGridSpec(
        num_scalar_prefetch=3,
        grid=(n_tiles,),
        in_specs=[pl.BlockSpec(memory_space=pl.ANY),
                  pl.BlockSpec((1, d, 2 * ff), lambda i, te, lv, s: (te[i], 0, 0)),
                  pl.BlockSpec((1, ff, d), lambda i, te, lv, s: (te[i], 0, 0))],
        out_specs=pl.BlockSpec((tile, d), lambda i, te, lv, s: (i, 0)),
        scratch_shapes=[pltpu.VMEM((2, tile, d), F32),
                        pltpu.VMEM((d, 2 * ff), BF16), pltpu.VMEM((ff, d), BF16),
                        pltpu.SemaphoreType.DMA((2,))],
    )
    return pl.pallas_call(
        functools.partial(_expert_kernel, tile=tile, ff=ff),
        out_shape=jax.ShapeDtypeStruct((n_tiles * tile, d), F32),
        grid_spec=grid_spec,
        compiler_params=_cparams("arbitrary"),
        name="experts",
    )(tile_expert, live, src_tok, h, wug, wd)


def _combine_kernel(pos_ref, x_ref, wts_ref, g_ref, y_hbm, *rest, emit_x, n_tokens):
    *out_refs, ybuf, sem = rest
    tm = x_ref.shape[0]
    i = pl.program_id(0)
    slot = i % 2

    def start_gather(t, s):
        for static_s in range(2):
            @pl.when(s == static_s)
            def _():
                for r in range(tm):
                    for k in range(TOP_K):
                        pltpu.make_async_copy(y_hbm.at[pl.ds(pos_ref[k * n_tokens + t * tm + r], 1)],
                                              ybuf.at[static_s, k, pl.ds(r, 1)], sem.at[static_s]).start()

    @pl.when(i == 0)
    def _():
        start_gather(0, 0)

    for k in range(TOP_K):
        pltpu.make_async_copy(y_hbm.at[pl.ds(0, tm)], ybuf.at[slot, k], sem.at[slot]).wait()

    @pl.when(i + 1 < pl.num_programs(0))
    def _():
        start_gather(i + 1, 1 - slot)

    wts = wts_ref[...]
    x_new = x_ref[...] + wts[:, 0:1] * ybuf[slot, 0] + wts[:, 1:2] * ybuf[slot, 1]
    if emit_x:
        out_refs[0][...] = x_new
    out_refs[-1][...] = _rms(x_new, g_ref[...]).astype(out_refs[-1].dtype)


def _combine(x, y_sorted, pos, wts, g, emit_x, norm_dtype, tm):
    n, d = x.shape
    tm = min(tm, n)
    row = pl.BlockSpec((tm, d), lambda i, p: (i, 0))
    out_shape = [jax.ShapeDtypeStruct((n, d), norm_dtype)]
    out_specs = [row]
    if emit_x:
        out_shape.insert(0, jax.ShapeDtypeStruct((n, d), F32))
        out_specs.insert(0, row)
    grid_spec = pltpu.PrefetchScalarGridSpec(
        num_scalar_prefetch=1,
        grid=(n // tm,),
        in_specs=[row, pl.BlockSpec((tm, LANES), lambda i, p: (i, 0)), pl.BlockSpec((1, d), lambda i, p: (0, 0)),
                  pl.BlockSpec(memory_space=pl.ANY)],
        out_specs=tuple(out_specs),
        scratch_shapes=[pltpu.VMEM((2, TOP_K, tm, d), F32), pltpu.SemaphoreType.DMA((2,))],
    )
    return pl.pallas_call(
        functools.partial(_combine_kernel, emit_x=emit_x, n_tokens=n),
        out_shape=tuple(out_shape),
        grid_spec=grid_spec,
        compiler_params=_cparams("arbitrary"),
        name="combine",
    )(pos, x, wts, g.reshape(1, d), y_sorted)


def _hier_moe(x_mid, g_ffn, w_group, b_group, w_expert, b_expert, w_up_gate, w_down, g_next, emit_x, norm_dtype):
    n, d = x_mid.shape
    n_experts = w_expert.shape[1]
    pad = LANES - N_GROUPS - n_experts
    w_r = jnp.concatenate([w_group, w_expert, jnp.zeros((d, pad), F32)], axis=1)
    b_r = jnp.concatenate([b_group, b_expert, jnp.zeros((pad,), F32)]).reshape(1, LANES)
    h, ids, wts = _router(x_mid, g_ffn, w_r, b_r, N_GROUPS, n_experts, tm=256)
    plan = _moe_plan(ids, n_experts, MOE_TILE)
    y_sorted = _experts(h, plan, w_up_gate, w_down, MOE_TILE)
    return _combine(x_mid, y_sorted, plan[3], wts, g_next, emit_x, norm_dtype, tm=256)


def _rope_tables(seq, rope_dim):
    pos = jnp.arange(seq, dtype=F32)
    inv_freq = ROPE_THETA ** (-jnp.arange(0, rope_dim, 2, dtype=F32) / rope_dim)
    ang = pos[:, None] * inv_freq[None, :]
    cos, sin = jnp.cos(ang), jnp.sin(ang)
    zero = jnp.zeros_like(cos)
    return (jnp.concatenate([cos, cos, zero, zero], axis=1),
            jnp.concatenate([-sin, sin, zero, zero], axis=1))


def kernel(x, l0_g_mix, l0_w_in, l0_conv_w, l0_g_q_lat, l0_w_uq, l0_g_kv_lat, l0_w_ukv, l0_w_out, l0_g_ffn, l0_w_group, l0_b_group, l0_w_expert, l0_b_expert, l0_w_up_gate, l0_w_down, l1_g_mix, l1_w_qkv, l1_lam_q1, l1_lam_k1, l1_lam_q2, l1_lam_k2, l1_g_subln, l1_w_out, l1_g_ffn, l1_w_group, l1_b_group, l1_w_expert, l1_b_expert, l1_w_up_gate, l1_w_down, g_final):
    batch, seq, d = x.shape
    n = batch * seq
    x2 = x.reshape(n, d)
    cw = l0_conv_w.shape[1]
    qr, kvr = l0_g_q_lat.shape[0], l0_g_kv_lat.shape[0]
    heads = (d - cw) // LANES
    rope = l0_w_in.shape[1] - 3 * cw - qr - kvr
    nope = l0_w_uq.shape[1] // heads - rope
    assert rope == LANES // 2 and nope == LANES

    half = rope // 2
    kr0 = 3 * cw + qr + kvr
    t1, t2 = l0_w_in[:, kr0:kr0 + half], l0_w_in[:, kr0 + half:kr0 + rope]
    used_cols = kr0 + 2 * rope
    in_width = -(-used_cols // 512) * 512
    w_in = jnp.concatenate([l0_w_in[:, :kr0], t1, t2, t2, t1, jnp.zeros((d, in_width - used_cols), F32)], axis=1)
    proj = _norm_matmul(x2, l0_g_mix, w_in.astype(BF16), F32, tm=1024, tn=512)
    y_conv = _gated_conv(proj, l0_conv_w, seq, tm=512)

    scale_mla = (nope + rope) ** -0.5
    wq = l0_w_uq.reshape(qr, heads, nope + rope) * scale_mla
    wq = jnp.concatenate([wq[..., :nope], wq[..., nope:nope + half], wq[..., nope + half:],
                          wq[..., nope + half:], wq[..., nope:nope + half]], axis=-1)
    wq = wq.reshape(qr, heads * 2 * LANES).astype(BF16)
    cos_t, sin_t = _rope_tables(seq, rope)
    q, k, v = _mla_prep(proj, 3 * cw, l0_g_q_lat, l0_g_kv_lat, wq, l0_w_ukv.astype(BF16), cos_t, sin_t,
                        heads, seq, tm=256)
    y_mla = _mla_flash(q, k, v, batch, seq, tq=512)
    x_mid = _outproj(y_conv, y_mla, 0, l0_w_out.astype(BF16), x2, tm=512)
    x1, h1 = _hier_moe(x_mid, l0_g_ffn, l0_w_group, l0_b_group, l0_w_expert, l0_b_expert,
                       l0_w_up_gate, l0_w_down, l1_g_mix, True, BF16)

    dv = l1_g_subln.shape[0]
    dheads = d // dv
    dh = dv // 2
    lam_init = 0.8 - 0.6 * math.exp(-0.3 * 1)
    lam = (jnp.exp(jnp.sum(l1_lam_q1 * l1_lam_k1)) - jnp.exp(jnp.sum(l1_lam_q2 * l1_lam_k2)) + lam_init)
    w_qkv = jnp.concatenate([l1_w_qkv[:, :d] * dh ** -0.5, l1_w_qkv[:, d:]], axis=1).astype(BF16)
    qkv = _matmul(h1, w_qkv, BF16, tm=1024, tn=512)
    o = _diff_flash(qkv, lam, l1_g_subln, batch, seq, dheads, 1.0 - lam_init, tq=512)
    x_mid = _outproj(o, o, 1, l1_w_out.astype(BF16), x1, tm=512)
    (out,) = _hier_moe(x_mid, l1_g_ffn, l1_w_group, l1_b_group, l1_w_expert, l1_b_expert,
                       l1_w_up_gate, l1_w_down, g_final, False, F32)
    return out.reshape(batch, seq, d)
```

```python
import functools
import math

import jax
import jax.numpy as jnp
import numpy as np
from jax import lax
from jax.experimental import pallas as pl
from jax.experimental.pallas import tpu as pltpu

F32 = jnp.float32
BF16 = jnp.bfloat16

NORM_EPS = 1e-6
ROPE_THETA = 10000.0
LANES = 128
SUBLANES = 8
MOE_TILE = 256
NORM_ROWS = 256
CAST_ROWS = 256
N_GROUPS = 4
TOP_K = 2
VMEM_LIMIT_BYTES = 52 * 1024 * 1024


def _cparams(*sem):
    return pltpu.CompilerParams(dimension_semantics=sem, vmem_limit_bytes=VMEM_LIMIT_BYTES)


def _rms(xf, g):
    ms = jnp.mean(xf * xf, axis=-1, keepdims=True)
    return xf * lax.rsqrt(ms + NORM_EPS) * g


def _norm_matmul_kernel(x_ref, g_ref, w_ref, o_ref, h_ref):
    @pl.when(pl.program_id(1) == 0)
    def _():
        rows = min(NORM_ROWS, x_ref.shape[0])
        for r0 in range(0, x_ref.shape[0], rows):
            h_ref[r0:r0 + rows, :] = _rms(x_ref[r0:r0 + rows, :], g_ref[...]).astype(BF16)

    o_ref[...] = jnp.dot(h_ref[...], w_ref[...], preferred_element_type=F32).astype(o_ref.dtype)


def _norm_matmul(x, g, w, out_dtype, tm, tn):
    n, k = x.shape
    nn = w.shape[1]
    tm, tn = min(tm, n), min(tn, nn)
    return pl.pallas_call(
        _norm_matmul_kernel,
        out_shape=jax.ShapeDtypeStruct((n, nn), out_dtype),
        grid=(n // tm, nn // tn),
        in_specs=[pl.BlockSpec((tm, k), lambda i, j: (i, 0)),
                  pl.BlockSpec((1, k), lambda i, j: (0, 0)),
                  pl.BlockSpec((k, tn), lambda i, j: (0, j))],
        out_specs=pl.BlockSpec((tm, tn), lambda i, j: (i, j)),
        scratch_shapes=[pltpu.VMEM((tm, k), BF16)],
        compiler_params=_cparams("parallel", "arbitrary"),
        name="norm_matmul",
    )(x, g.reshape(1, k), w)


def _matmul_kernel(x_ref, w_ref, o_ref):
    o_ref[...] = jnp.dot(x_ref[...], w_ref[...], preferred_element_type=F32).astype(o_ref.dtype)


def _matmul(x, w, out_dtype, tm, tn):
    n, k = x.shape
    nn = w.shape[1]
    tm, tn = min(tm, n), min(tn, nn)
    return pl.pallas_call(
        _matmul_kernel,
        out_shape=jax.ShapeDtypeStruct((n, nn), out_dtype),
        grid=(n // tm, nn // tn),
        in_specs=[pl.BlockSpec((tm, k), lambda i, j: (i, 0)),
                  pl.BlockSpec((k, tn), lambda i, j: (0, j))],
        out_specs=pl.BlockSpec((tm, tn), lambda i, j: (i, j)),
        compiler_params=_cparams("parallel", "arbitrary"),
        name="matmul",
    )(x, w)


def _outproj_kernel(ya_ref, yb_ref, w_ref, x_ref, o_ref):
    ka = ya_ref.shape[1]
    acc = jnp.dot(ya_ref[...], w_ref[:ka, :], preferred_element_type=F32)
    acc = acc + jnp.dot(yb_ref[...], w_ref[ka:, :], preferred_element_type=F32)
    o_ref[...] = x_ref[...] + acc


def _outproj(ya, yb, yb_col, w, x, tm):
    n, d = x.shape
    ka = d // 2
    tm = min(tm, n)
    return pl.pallas_call(
        _outproj_kernel,
        out_shape=jax.ShapeDtypeStruct((n, d), F32),
        grid=(n // tm,),
        in_specs=[pl.BlockSpec((tm, ka), lambda i: (i, 0)),
                  pl.BlockSpec((tm, ka), lambda i: (i, yb_col)),
                  pl.BlockSpec((d, d), lambda i: (0, 0)),
                  pl.BlockSpec((tm, d), lambda i: (i, 0))],
        out_specs=pl.BlockSpec((tm, d), lambda i: (i, 0)),
        compiler_params=_cparams("parallel"),
        name="outproj",
    )(ya, yb, w, x)


def _conv_kernel(b_ref, c_ref, u_ref, hc_ref, hu_ref, w_ref, o_ref, *, tiles_per_seq):
    tm = c_ref.shape[0]
    first = (pl.program_id(0) % tiles_per_seq) == 0
    v = c_ref[...] * u_ref[...]
    halo = jnp.where(first, 0.0, hc_ref[...] * hu_ref[...])
    row = lax.broadcasted_iota(jnp.int32, v.shape, 0)
    v1 = jnp.where(row == 0, halo[7:8, :], pltpu.roll(v, 1, axis=0))
    v2 = jnp.where(row == 0, halo[6:7, :],
                   jnp.where(row == 1, halo[7:8, :], pltpu.roll(v, 2, axis=0)))
    w = w_ref[...]
    conv = w[2:3, :] * v + w[1:2, :] * v1 + w[0:1, :] * v2
    o_ref[...] = (b_ref[...] * conv).astype(o_ref.dtype)


def _gated_conv(proj, conv_w, seq, tm):
    n = proj.shape[0]
    cw = conv_w.shape[1]
    tm = min(tm, seq)
    hb = tm // 8
    return pl.pallas_call(
        functools.partial(_conv_kernel, tiles_per_seq=seq // tm),
        out_shape=jax.ShapeDtypeStruct((n, cw), BF16),
        grid=(n // tm,),
        in_specs=[pl.BlockSpec((tm, cw), lambda i: (i, 0)),
                  pl.BlockSpec((tm, cw), lambda i: (i, 1)),
                  pl.BlockSpec((tm, cw), lambda i: (i, 2)),
                  pl.BlockSpec((8, cw), lambda i: (jnp.maximum(i * hb - 1, 0), 1)),
                  pl.BlockSpec((8, cw), lambda i: (jnp.maximum(i * hb - 1, 0), 2)),
                  pl.BlockSpec((conv_w.shape[0], cw), lambda i: (0, 0))],
        out_specs=pl.BlockSpec((tm, cw), lambda i: (i, 0)),
        compiler_params=_cparams("parallel"),
        name="gated_conv",
    )(proj, proj, proj, proj, proj, conv_w)


def _rope_lanes(a, cos_t, sin_t):
    return a * cos_t + pltpu.roll(a, LANES // 2, axis=1) * sin_t


def _mla_prep_kernel(cq_ref, ckv_ref, kr_ref, gq_ref, gkv_ref, wq_ref, wkv_ref, cos_ref, sin_ref,
                     q_ref, k_ref, v_ref, *, heads):
    cqn = _rms(cq_ref[...], gq_ref[...]).astype(BF16)
    ckvn = _rms(ckv_ref[...], gkv_ref[...]).astype(BF16)
    cos_t, sin_t = cos_ref[...], sin_ref[...]
    q = jnp.dot(cqn, wq_ref[...], preferred_element_type=F32)
    kv = jnp.dot(ckvn, wkv_ref[...], preferred_element_type=F32)
    k_rope = _rope_lanes(kr_ref[...], cos_t, sin_t).astype(BF16)
    for h in range(heads):
        c0 = h * 2 * LANES
        q_ref[h, :, :LANES] = q[:, c0:c0 + LANES].astype(BF16)
        q_ref[h, :, LANES:] = _rope_lanes(q[:, c0 + LANES:c0 + 2 * LANES], cos_t, sin_t).astype(BF16)
        k_ref[h, :, :LANES] = kv[:, c0:c0 + LANES].astype(BF16)
        k_ref[h, :, LANES:] = k_rope
        v_ref[h] = kv[:, c0 + LANES:c0 + 2 * LANES].astype(BF16)


def _mla_prep(proj, col0, g_q, g_kv, wq, wkv, cos_t, sin_t, heads, seq, tm):
    n = proj.shape[0]
    qr, kvr = g_q.shape[0], g_kv.shape[0]
    tm = min(tm, seq)
    ps = seq // tm
    assert col0 % qr == 0 and (col0 + qr) % kvr == 0 and (col0 + qr + kvr) % LANES == 0
    return pl.pallas_call(
        functools.partial(_mla_prep_kernel, heads=heads),
        out_shape=(jax.ShapeDtypeStruct((heads, n, 2 * LANES), BF16),
                   jax.ShapeDtypeStruct((heads, n, 2 * LANES), BF16),
                   jax.ShapeDtypeStruct((heads, n, LANES), BF16)),
        grid=(n // tm,),
        in_specs=[pl.BlockSpec((tm, qr), lambda i: (i, col0 // qr)),
                  pl.BlockSpec((tm, kvr), lambda i: (i, (col0 + qr) // kvr)),
                  pl.BlockSpec((tm, LANES), lambda i: (i, (col0 + qr + kvr) // LANES)),
                  pl.BlockSpec((1, qr), lambda i: (0, 0)),
                  pl.BlockSpec((1, kvr), lambda i: (0, 0)),
                  pl.BlockSpec(wq.shape, lambda i: (0, 0)),
                  pl.BlockSpec(wkv.shape, lambda i: (0, 0)),
                  pl.BlockSpec((tm, LANES), lambda i: (i % ps, 0)),
                  pl.BlockSpec((tm, LANES), lambda i: (i % ps, 0))],
        out_specs=(pl.BlockSpec((heads, tm, 2 * LANES), lambda i: (0, i, 0)),
                   pl.BlockSpec((heads, tm, 2 * LANES), lambda i: (0, i, 0)),
                   pl.BlockSpec((heads, tm, LANES), lambda i: (0, i, 0))),
        compiler_params=_cparams("parallel"),
        name="mla_prep",
    )(proj, proj, proj, g_q.reshape(1, qr), g_kv.reshape(1, kvr), wq, wkv, cos_t, sin_t)


def _online_step(s, vblk, carry):
    m, l, acc = carry
    m_new = jnp.maximum(m, jnp.max(s, axis=1, keepdims=True))
    alpha = jnp.exp(m - m_new)
    p = jnp.exp(s - m_new)
    l = alpha * l + jnp.sum(p, axis=1, keepdims=True)
    acc = alpha * acc + jnp.dot(p.astype(BF16), vblk, preferred_element_type=F32)
    return m_new, l, acc


def _qk(q, k):
    return lax.dot_general(q, k, (((1,), (1,)), ((), ())), preferred_element_type=F32)


def _softmax_init(tq, dv):
    return (jnp.full((tq, 1), -1e30, F32), jnp.zeros((tq, 1), F32), jnp.zeros((tq, dv), F32))


def _mla_flash_kernel(q_ref, k_ref, v_ref, o_ref):
    tq = q_ref.shape[1]
    qi = pl.program_id(2)
    q = q_ref[0]

    def body(j, carry):
        off = pl.multiple_of(j * tq, tq)
        return _online_step(_qk(q, k_ref[0, pl.ds(off, tq), :]), v_ref[0, pl.ds(off, tq), :], carry)

    carry = lax.fori_loop(0, qi, body, _softmax_init(tq, v_ref.shape[2]))
    off = pl.multiple_of(qi * tq, tq)
    causal = (lax.broadcasted_iota(jnp.int32, (tq, tq), 1) <= lax.broadcasted_iota(jnp.int32, (tq, tq), 0))
    s = jnp.where(causal, _qk(q, k_ref[0, pl.ds(off, tq), :]), -jnp.inf)
    _, l, acc = _online_step(s, v_ref[0, pl.ds(off, tq), :], carry)
    o_ref[...] = (acc / l).astype(o_ref.dtype)


def _mla_flash(q, k, v, batch, seq, tq):
    heads, n, dk = q.shape
    dv = v.shape[2]
    tq = min(tq, seq)
    nq = seq // tq
    return pl.pallas_call(
        _mla_flash_kernel,
        out_shape=jax.ShapeDtypeStruct((n, heads * dv), BF16),
        grid=(batch, heads, nq),
        in_specs=[pl.BlockSpec((1, tq, dk), lambda b, h, i: (h, b * nq + i, 0)),
                  pl.BlockSpec((1, seq, dk), lambda b, h, i: (h, b, 0)),
                  pl.BlockSpec((1, seq, dv), lambda b, h, i: (h, b, 0))],
        out_specs=pl.BlockSpec((tq, dv), lambda b, h, i: (b * nq + i, h)),
        compiler_params=_cparams("parallel", "parallel", "arbitrary"),
        name="mla_flash",
    )(q, k, v)


def _diff_flash_kernel(lam_ref, slopes_ref, q_ref, k_ref, v_ref, g_ref, o_ref, *, out_scale):
    tq = q_ref.shape[0]
    dh = q_ref.shape[1] // 2
    dv = v_ref.shape[1]
    qi = pl.program_id(2)
    slope = slopes_ref[0, pl.program_id(1)]
    q1, q2 = q_ref[:, :dh], q_ref[:, dh:]
    col = lax.broadcasted_iota(jnp.int32, (1, tq), 1).astype(F32)

    def scores(off, qpart, lo):
        return _qk(qpart, k_ref[pl.ds(off, tq), lo:lo + dh])

    def body(j, carry):
        c1, c2 = carry
        off = pl.multiple_of(j * tq, tq)
        bias = slope * (col + (off - qi * tq).astype(F32))
        vblk = v_ref[pl.ds(off, tq), :]
        c1 = _online_step(scores(off, q1, 0) + bias, vblk, c1)
        c2 = _online_step(scores(off, q2, dh) + bias, vblk, c2)
        return c1, c2

    c1, c2 = lax.fori_loop(0, qi, body, (_softmax_init(tq, dv), _softmax_init(tq, dv)))
    off = pl.multiple_of(qi * tq, tq)
    causal = (lax.broadcasted_iota(jnp.int32, (tq, tq), 1) <= lax.broadcasted_iota(jnp.int32, (tq, tq), 0))
    bias = slope * col
    vblk = v_ref[pl.ds(off, tq), :]
    _, l1, a1 = _online_step(jnp.where(causal, scores(off, q1, 0) + bias, -jnp.inf), vblk, c1)
    _, l2, a2 = _online_step(jnp.where(causal, scores(off, q2, dh) + bias, -jnp.inf), vblk, c2)
    o = a1 / l1 - lam_ref[0, 0] * (a2 / l2)
    o_ref[...] = (_rms(o, g_ref[...]) * out_scale).astype(o_ref.dtype)


def _diff_flash(qkv, lam, g_subln, batch, seq, heads, out_scale, tq):
    n = qkv.shape[0]
    dv = g_subln.shape[0]
    tq = min(tq, seq)
    nq = seq // tq
    slopes = jnp.asarray((2.0 ** (-8.0 * np.arange(1, heads + 1) / heads)).astype(np.float32)).reshape(1, heads)
    return pl.pallas_call(
        functools.partial(_diff_flash_kernel, out_scale=out_scale),
        out_shape=jax.ShapeDtypeStruct((n, heads * dv), BF16),
        grid=(batch, heads, nq),
        in_specs=[pl.BlockSpec(memory_space=pltpu.SMEM),
                  pl.BlockSpec(memory_space=pltpu.SMEM),
                  pl.BlockSpec((tq, dv), lambda b, h, i: (b * nq + i, h)),
                  pl.BlockSpec((seq, dv), lambda b, h, i: (b, heads + h)),
                  pl.BlockSpec((seq, dv), lambda b, h, i: (b, 2 * heads + h)),
                  pl.BlockSpec((1, dv), lambda b, h, i: (0, 0))],
        out_specs=pl.BlockSpec((tq, dv), lambda b, h, i: (b * nq + i, h)),
        compiler_params=_cparams("parallel", "parallel", "arbitrary"),
        name="diff_flash",
    )(lam.reshape(1, 1), slopes, qkv, qkv, qkv, g_subln.reshape(1, dv))


def _router_kernel(x_ref, g_ref, w_ref, b_ref, h_ref, ids_ref, wts_ref, *, n_groups, n_experts):
    h = _rms(x_ref[...], g_ref[...])
    h_ref[...] = h
    logits = jnp.dot(h, w_ref[...], preferred_element_type=F32, precision=lax.Precision.HIGHEST) + b_ref[...]
    lane = lax.broadcasted_iota(jnp.int32, logits.shape, 1).astype(F32)
    neg = -jnp.inf
    per_group = n_experts // n_groups
    gl = jnp.where(lane < n_groups, logits, neg)
    gmax = jnp.max(gl, axis=1, keepdims=True)
    g_prob = 1.0 / jnp.sum(jnp.exp(gl - gmax), axis=1, keepdims=True)
    g_idx = jnp.min(jnp.where(gl == gmax, lane, float(LANES)), axis=1, keepdims=True)
    lo = n_groups + g_idx * per_group
    el = jnp.where((lane >= lo) & (lane < lo + per_group), logits, neg)
    m1 = jnp.max(el, axis=1, keepdims=True)
    i1 = jnp.min(jnp.where(el == m1, lane, float(LANES)), axis=1, keepdims=True)
    el2 = jnp.where(lane == i1, neg, el)
    m2 = jnp.max(el2, axis=1, keepdims=True)
    i2 = jnp.min(jnp.where(el2 == m2, lane, float(LANES)), axis=1, keepdims=True)
    e2 = jnp.exp(m2 - m1)
    w1 = g_prob / (1.0 + e2)
    w2 = g_prob * e2 / (1.0 + e2)
    ids = jnp.where(lane == 0.0, i1 - n_groups, jnp.where(lane == 1.0, i2 - n_groups, 0.0))
    ids_ref[...] = ids.astype(jnp.int32)
    wts_ref[...] = jnp.where(lane == 0.0, w1, jnp.where(lane == 1.0, w2, 0.0))


def _router(x, g, w_r, b_r, n_groups, n_experts, tm):
    n, d = x.shape
    tm = min(tm, n)
    return pl.pallas_call(
        functools.partial(_router_kernel, n_groups=n_groups, n_experts=n_experts),
        out_shape=(jax.ShapeDtypeStruct((n, d), F32),
                   jax.ShapeDtypeStruct((n, LANES), jnp.int32),
                   jax.ShapeDtypeStruct((n, LANES), F32)),
        grid=(n // tm,),
        in_specs=[pl.BlockSpec((tm, d), lambda i: (i, 0)),
                  pl.BlockSpec((1, d), lambda i: (0, 0)),
                  pl.BlockSpec((d, LANES), lambda i: (0, 0)),
                  pl.BlockSpec((1, LANES), lambda i: (0, 0))],
        out_specs=(pl.BlockSpec((tm, d), lambda i: (i, 0)),
                   pl.BlockSpec((tm, LANES), lambda i: (i, 0)),
                   pl.BlockSpec((tm, LANES), lambda i: (i, 0))),
        compiler_params=_cparams("parallel"),
        name="router",
    )(x, g.reshape(1, d), w_r, b_r)


def _moe_plan(ids, n_experts, tile):
    n = ids.shape[0]
    na = TOP_K * n
    n_tiles = na // tile + n_experts
    ef = ids[:, :TOP_K].T.reshape(na)
    onehot = (ef[:, None] == jnp.arange(n_experts, dtype=jnp.int32)[None, :]).astype(jnp.int32)
    csum = jnp.cumsum(onehot, axis=0)
    rank = jnp.sum(csum * onehot, axis=1) - 1
    counts = csum[-1]
    tiles_per = (counts + tile - 1) // tile
    tile_end = jnp.cumsum(tiles_per)
    tile_start = tile_end - tiles_per
    pos = (tile_start[ef] * tile + rank).astype(jnp.int32)
    a = jnp.arange(na, dtype=jnp.int32)
    src_tok = jnp.zeros((n_tiles * tile,), jnp.int32).at[pos].set(a % n)
    t = jnp.arange(n_tiles, dtype=jnp.int32)
    used = tile_end[-1]
    tile_expert = jnp.sum((tile_end[None, :] <= jnp.minimum(t, used - 1)[:, None]).astype(jnp.int32), axis=1)
    live = (t < used).astype(jnp.int32)
    return tile_expert, live, src_tok, pos


def _expert_kernel(te_ref, live_ref, src_ref, h_hbm, wug_ref, wd_ref, o_ref, xbuf, wug_bf, wd_bf, gsem, *, tile, ff):
    i = pl.program_id(0)
    slot = i % 2
    prev = jnp.maximum(i - 1, 0)

    def start_gather(t, s):
        for static_s in range(2):
            @pl.when(s == static_s)
            def _():
                for r in range(tile):
                    pltpu.make_async_copy(h_hbm.at[pl.ds(src_ref[t * tile + r], 1)],
                                          xbuf.at[static_s, pl.ds(r, 1)], gsem.at[static_s]).start()

    def wait_gather(s):
        pltpu.make_async_copy(h_hbm.at[pl.ds(0, tile)], xbuf.at[s], gsem.at[s]).wait()

    @pl.when(i == 0)
    def _():
        start_gather(0, 0)

    @pl.when(live_ref[i] > 0)
    def _():
        @pl.when((i == 0) | (te_ref[i] != te_ref[prev]))
        def _():
            for r0 in range(0, wug_ref.shape[1], CAST_ROWS):
                wug_bf[r0:r0 + CAST_ROWS, :] = wug_ref[0, r0:r0 + CAST_ROWS, :].astype(BF16)
            for r0 in range(0, wd_ref.shape[1], CAST_ROWS):
                wd_bf[r0:r0 + CAST_ROWS, :] = wd_ref[0, r0:r0 + CAST_ROWS, :].astype(BF16)

        wait_gather(slot)
        start_gather(i + 1, 1 - slot)
        gu = jnp.dot(xbuf[slot].astype(BF16), wug_bf[...], preferred_element_type=F32)
        gate, up = gu[:, :ff], gu[:, ff:]
        act = (gate * jax.nn.sigmoid(gate) * up).astype(BF16)
        o_ref[...] = jnp.dot(act, wd_bf[...], preferred_element_type=F32)

    @pl.when(live_ref[i] == 0)
    def _():
        @pl.when((i > 0) & (live_ref[prev] > 0))
        def _():
            wait_gather(slot)
        o_ref[...] = jnp.zeros_like(o_ref)


def _experts(h, plan, wug, wd, tile):
    tile_expert, live, src_tok, _ = plan
    d, ff = wd.shape[2], wd.shape[1]
    n_tiles = tile_expert.shape[0]
    grid_spec = pltpu.PrefetchScalarGridSpec(
        num_scalar_prefetch=3,
        grid=(n_tiles,),
        in_specs=[pl.BlockSpec(memory_space=pl.ANY),
                  pl.BlockSpec((1, d, 2 * ff), lambda i, te, lv, s: (te[i], 0, 0)),
                  pl.BlockSpec((1, ff, d), lambda i, te, lv, s: (te[i], 0, 0))],
        out_specs=pl.BlockSpec((tile, d), lambda i, te, lv, s: (i, 0)),
        scratch_shapes=[pltpu.VMEM((2, tile, d), F32),
                        pltpu.VMEM((d, 2 * ff), BF16), pltpu.VMEM((ff, d), BF16),
                        pltpu.SemaphoreType.DMA((2,))],
    )
    return pl.pallas_call(
        functools.partial(_expert_kernel, tile=tile, ff=ff),
        out_shape=jax.ShapeDtypeStruct((n_tiles * tile, d), F32),
        grid_spec=grid_spec,
        compiler_params=_cparams("arbitrary"),
        name="experts",
    )(tile_expert, live, src_tok, h, wug, wd)


def _combine_kernel(pos_ref, x_ref, wts_ref, g_ref, y_hbm, *rest, emit_x, n_tokens):
    *out_refs, ybuf, sem = rest
    tm = x_ref.shape[0]
    i = pl.program_id(0)
    slot = i % 2

    def start_gather(t, s):
        for static_s in range(2):
            @pl.when(s == static_s)
            def _():
                for r in range(tm):
                    for k in range(TOP_K):
                        pltpu.make_async_copy(y_hbm.at[pl.ds(pos_ref[k * n_tokens + t * tm + r], 1)],
                                              ybuf.at[static_s, k, pl.ds(r, 1)], sem.at[static_s]).start()

    @pl.when(i == 0)
    def _():
        start_gather(0, 0)

    for k in range(TOP_K):
        pltpu.make_async_copy(y_hbm.at[pl.ds(0, tm)], ybuf.at[slot, k], sem.at[slot]).wait()

    @pl.when(i + 1 < pl.num_programs(0))
    def _():
        start_gather(i + 1, 1 - slot)

    wts = wts_ref[...]
    x_new = x_ref[...] + wts[:, 0:1] * ybuf[slot, 0] + wts[:, 1:2] * ybuf[slot, 1]
    if emit_x:
        out_refs[0][...] = x_new
    out_refs[-1][...] = _rms(x_new, g_ref[...]).astype(out_refs[-1].dtype)


def _combine(x, y_sorted, pos, wts, g, emit_x, norm_dtype, tm):
    n, d = x.shape
    tm = min(tm, n)
    row = pl.BlockSpec((tm, d), lambda i, p: (i, 0))
    out_shape = [jax.ShapeDtypeStruct((n, d), norm_dtype)]
    out_specs = [row]
    if emit_x:
        out_shape.insert(0, jax.ShapeDtypeStruct((n, d), F32))
        out_specs.insert(0, row)
    grid_spec = pltpu.PrefetchScalarGridSpec(
        num_scalar_prefetch=1,
        grid=(n // tm,),
        in_specs=[row, pl.BlockSpec((tm, LANES), lambda i, p: (i, 0)), pl.BlockSpec((1, d), lambda i, p: (0, 0)),
                  pl.BlockSpec(memory_space=pl.ANY)],
        out_specs=tuple(out_specs),
        scratch_shapes=[pltpu.VMEM((2, TOP_K, tm, d), F32), pltpu.SemaphoreType.DMA((2,))],
    )
    return pl.pallas_call(
        functools.partial(_combine_kernel, emit_x=emit_x, n_tokens=n),
        out_shape=tuple(out_shape),
        grid_spec=grid_spec,
        compiler_params=_cparams("arbitrary"),
        name="combine",
    )(pos, x, wts, g.reshape(1, d), y_sorted)


def _hier_moe(x_mid, g_ffn, w_group, b_group, w_expert, b_expert, w_up_gate, w_down, g_next, emit_x, norm_dtype):
    n, d = x_mid.shape
    n_experts = w_expert.shape[1]
    pad = LANES - N_GROUPS - n_experts
    w_r = jnp.concatenate([w_group, w_expert, jnp.zeros((d, pad), F32)], axis=1)
    b_r = jnp.concatenate([b_group, b_expert, jnp.zeros((pad,), F32)]).reshape(1, LANES)
    h, ids, wts = _router(x_mid, g_ffn, w_r, b_r, N_GROUPS, n_experts, tm=256)
    plan = _moe_plan(ids, n_experts, MOE_TILE)
    y_sorted = _experts(h, plan, w_up_gate, w_down, MOE_TILE)
    return _combine(x_mid, y_sorted, plan[3], wts, g_next, emit_x, norm_dtype, tm=256)


def _rope_tables(seq, rope_dim):
    pos = jnp.arange(seq, dtype=F32)
    inv_freq = ROPE_THETA ** (-jnp.arange(0, rope_dim, 2, dtype=F32) / rope_dim)
    ang = pos[:, None] * inv_freq[None, :]
    cos, sin = jnp.cos(ang), jnp.sin(ang)
    zero = jnp.zeros_like(cos)
    return (jnp.concatenate([cos, cos, zero, zero], axis=1),
            jnp.concatenate([-sin, sin, zero, zero], axis=1))


def kernel(x, l0_g_mix, l0_w_in, l0_conv_w, l0_g_q_lat, l0_w_uq, l0_g_kv_lat, l0_w_ukv, l0_w_out, l0_g_ffn, l0_w_group, l0_b_group, l0_w_expert, l0_b_expert, l0_w_up_gate, l0_w_down, l1_g_mix, l1_w_qkv, l1_lam_q1, l1_lam_k1, l1_lam_q2, l1_lam_k2, l1_g_subln, l1_w_out, l1_g_ffn, l1_w_group, l1_b_group, l1_w_expert, l1_b_expert, l1_w_up_gate, l1_w_down, g_final):
    batch, seq, d = x.shape
    n = batch * seq
    x2 = x.reshape(n, d)
    cw = l0_conv_w.shape[1]
    qr, kvr = l0_g_q_lat.shape[0], l0_g_kv_lat.shape[0]
    heads = (d - cw) // LANES
    rope = l0_w_in.shape[1] - 3 * cw - qr - kvr
    nope = l0_w_uq.shape[1] // heads - rope
    assert rope == LANES // 2 and nope == LANES

    half = rope // 2
    kr0 = 3 * cw + qr + kvr
    t1, t2 = l0_w_in[:, kr0:kr0 + half], l0_w_in[:, kr0 + half:kr0 + rope]
    used_cols = kr0 + 2 * rope
    in_width = -(-used_cols // 512) * 512
    w_in = jnp.concatenate([l0_w_in[:, :kr0], t1, t2, t2, t1, jnp.zeros((d, in_width - used_cols), F32)], axis=1)
    proj = _norm_matmul(x2, l0_g_mix, w_in.astype(BF16), F32, tm=1024, tn=512)
    y_conv = _gated_conv(proj, l0_conv_w, seq, tm=512)

    scale_mla = (nope + rope) ** -0.5
    wq = l0_w_uq.reshape(qr, heads, nope + rope) * scale_mla
    wq = jnp.concatenate([wq[..., :nope], wq[..., nope:nope + half], wq[..., nope + half:],
                          wq[..., nope + half:], wq[..., nope:nope + half]], axis=-1)
    wq = wq.reshape(qr, heads * 2 * LANES).astype(BF16)
    cos_t, sin_t = _rope_tables(seq, rope)
    q, k, v = _mla_prep(proj, 3 * cw, l0_g_q_lat, l0_g_kv_lat, wq, l0_w_ukv.astype(BF16), cos_t, sin_t,
                        heads, seq, tm=256)
    y_mla = _mla_flash(q, k, v, batch, seq, tq=1024)
    x_mid = _outproj(y_conv, y_mla, 0, l0_w_out.astype(BF16), x2, tm=512)
    x1, h1 = _hier_moe(x_mid, l0_g_ffn, l0_w_group, l0_b_group, l0_w_expert, l0_b_expert,
                       l0_w_up_gate, l0_w_down, l1_g_mix, True, BF16)

    dv = l1_g_subln.shape[0]
    dheads = d // dv
    dh = dv // 2
    lam_init = 0.8 - 0.6 * math.exp(-0.3 * 1)
    lam = (jnp.exp(jnp.sum(l1_lam_q1 * l1_lam_k1)) - jnp.exp(jnp.sum(l1_lam_q2 * l1_lam_k2)) + lam_init)
    w_qkv = jnp.concatenate([l1_w_qkv[:, :d] * dh ** -0.5, l1_w_qkv[:, d:]], axis=1).astype(BF16)
    qkv = _matmul(h1, w_qkv, BF16, tm=1024, tn=512)
    o = _diff_flash(qkv, lam, l1_g_subln, batch, seq, dheads, 1.0 - lam_init, tq=1024)
    x_mid = _outproj(o, o, 1, l1_w_out.astype(BF16), x1, tm=512)
    (out,) = _hier_moe(x_mid, l1_g_ffn, l1_w_group, l1_b_group, l1_w_expert, l1_b_expert,
                       l1_w_up_gate, l1_w_down, g_final, False, F32)
    return out.reshape(batch, seq, d)
```

```python
import functools
import math

import jax
import jax.numpy as jnp
import numpy as np
from jax import lax
from jax.experimental import pallas as pl
from jax.experimental.pallas import tpu as pltpu

F32 = jnp.float32
BF16 = jnp.bfloat16

NORM_EPS = 1e-6
ROPE_THETA = 10000.0
LANES = 128
SUBLANES = 8
MOE_TILE = 256
NORM_ROWS = 256
CAST_ROWS = 256
N_GROUPS = 4
TOP_K = 2
VMEM_LIMIT_BYTES = 52 * 1024 * 1024


def _cparams(*sem):
    return pltpu.CompilerParams(dimension_semantics=sem, vmem_limit_bytes=VMEM_LIMIT_BYTES)


def _rms(xf, g):
    ms = jnp.mean(xf * xf, axis=-1, keepdims=True)
    return xf * lax.rsqrt(ms + NORM_EPS) * g


def _norm_matmul_kernel(x_ref, g_ref, w_ref, o_ref, h_ref):
    @pl.when(pl.program_id(1) == 0)
    def _():
        rows = min(NORM_ROWS, x_ref.shape[0])
        for r0 in range(0, x_ref.shape[0], rows):
            h_ref[r0:r0 + rows, :] = _rms(x_ref[r0:r0 + rows, :], g_ref[...]).astype(BF16)

    o_ref[...] = jnp.dot(h_ref[...], w_ref[...], preferred_element_type=F32).astype(o_ref.dtype)


def _norm_matmul(x, g, w, out_dtype, tm, tn):
    n, k = x.shape
    nn = w.shape[1]
    tm, tn = min(tm, n), min(tn, nn)
    return pl.pallas_call(
        _norm_matmul_kernel,
        out_shape=jax.ShapeDtypeStruct((n, nn), out_dtype),
        grid=(n // tm, nn // tn),
        in_specs=[pl.BlockSpec((tm, k), lambda i, j: (i, 0)),
                  pl.BlockSpec((1, k), lambda i, j: (0, 0)),
                  pl.BlockSpec((k, tn), lambda i, j: (0, j))],
        out_specs=pl.BlockSpec((tm, tn), lambda i, j: (i, j)),
        scratch_shapes=[pltpu.VMEM((tm, k), BF16)],
        compiler_params=_cparams("parallel", "arbitrary"),
        name="norm_matmul",
    )(x, g.reshape(1, k), w)


def _matmul_kernel(x_ref, w_ref, o_ref):
    o_ref[...] = jnp.dot(x_ref[...], w_ref[...], preferred_element_type=F32).astype(o_ref.dtype)


def _matmul(x, w, out_dtype, tm, tn):
    n, k = x.shape
    nn = w.shape[1]
    tm, tn = min(tm, n), min(tn, nn)
    return pl.pallas_call(
        _matmul_kernel,
        out_shape=jax.ShapeDtypeStruct((n, nn), out_dtype),
        grid=(n // tm, nn // tn),
        in_specs=[pl.BlockSpec((tm, k), lambda i, j: (i, 0)),
                  pl.BlockSpec((k, tn), lambda i, j: (0, j))],
        out_specs=pl.BlockSpec((tm, tn), lambda i, j: (i, j)),
        compiler_params=_cparams("parallel", "arbitrary"),
        name="matmul",
    )(x, w)


def _outproj_kernel(ya_ref, yb_ref, w_ref, x_ref, o_ref):
    ka = ya_ref.shape[1]
    acc = jnp.dot(ya_ref[...], w_ref[:ka, :], preferred_element_type=F32)
    acc = acc + jnp.dot(yb_ref[...], w_ref[ka:, :], preferred_element_type=F32)
    o_ref[...] = x_ref[...] + acc


def _outproj(ya, yb, yb_col, w, x, tm):
    n, d = x.shape
    ka = d // 2
    tm = min(tm, n)
    return pl.pallas_call(
        _outproj_kernel,
        out_shape=jax.ShapeDtypeStruct((n, d), F32),
        grid=(n // tm,),
        in_specs=[pl.BlockSpec((tm, ka), lambda i: (i, 0)),
                  pl.BlockSpec((tm, ka), lambda i: (i, yb_col)),
                  pl.BlockSpec((d, d), lambda i: (0, 0)),
                  pl.BlockSpec((tm, d), lambda i: (i, 0))],
        out_specs=pl.BlockSpec((tm, d), lambda i: (i, 0)),
        compiler_params=_cparams("parallel"),
        name="outproj",
    )(ya, yb, w, x)


def _conv_kernel(b_ref, c_ref, u_ref, hc_ref, hu_ref, w_ref, o_ref, *, tiles_per_seq):
    tm = c_ref.shape[0]
    first = (pl.program_id(0) % tiles_per_seq) == 0
    v = c_ref[...] * u_ref[...]
    halo = jnp.where(first, 0.0, hc_ref[...] * hu_ref[...])
    row = lax.broadcasted_iota(jnp.int32, v.shape, 0)
    v1 = jnp.where(row == 0, halo[7:8, :], pltpu.roll(v, 1, axis=0))
    v2 = jnp.where(row == 0, halo[6:7, :],
                   jnp.where(row == 1, halo[7:8, :], pltpu.roll(v, 2, axis=0)))
    w = w_ref[...]
    conv = w[2:3, :] * v + w[1:2, :] * v1 + w[0:1, :] * v2
    o_ref[...] = (b_ref[...] * conv).astype(o_ref.dtype)


def _gated_conv(proj, conv_w, seq, tm):
    n = proj.shape[0]
    cw = conv_w.shape[1]
    tm = min(tm, seq)
    hb = tm // 8
    return pl.pallas_call(
        functools.partial(_conv_kernel, tiles_per_seq=seq // tm),
        out_shape=jax.ShapeDtypeStruct((n, cw), BF16),
        grid=(n // tm,),
        in_specs=[pl.BlockSpec((tm, cw), lambda i: (i, 0)),
                  pl.BlockSpec((tm, cw), lambda i: (i, 1)),
                  pl.BlockSpec((tm, cw), lambda i: (i, 2)),
                  pl.BlockSpec((8, cw), lambda i: (jnp.maximum(i * hb - 1, 0), 1)),
                  pl.BlockSpec((8, cw), lambda i: (jnp.maximum(i * hb - 1, 0), 2)),
                  pl.BlockSpec((conv_w.shape[0], cw), lambda i: (0, 0))],
        out_specs=pl.BlockSpec((tm, cw), lambda i: (i, 0)),
        compiler_params=_cparams("parallel"),
        name="gated_conv",
    )(proj, proj, proj, proj, proj, conv_w)


def _rope_lanes(a, cos_t, sin_t):
    return a * cos_t + pltpu.roll(a, LANES // 2, axis=1) * sin_t


def _mla_prep_kernel(cq_ref, ckv_ref, kr_ref, gq_ref, gkv_ref, wq_ref, wkv_ref, cos_ref, sin_ref,
                     q_ref, k_ref, v_ref, *, heads):
    cqn = _rms(cq_ref[...], gq_ref[...]).astype(BF16)
    ckvn = _rms(ckv_ref[...], gkv_ref[...]).astype(BF16)
    cos_t, sin_t = cos_ref[...], sin_ref[...]
    q = jnp.dot(cqn, wq_ref[...], preferred_element_type=F32)
    kv = jnp.dot(ckvn, wkv_ref[...], preferred_element_type=F32)
    k_rope = _rope_lanes(kr_ref[...], cos_t, sin_t).astype(BF16)
    for h in range(heads):
        c0 = h * 2 * LANES
        q_ref[h, :, :LANES] = q[:, c0:c0 + LANES].astype(BF16)
        q_ref[h, :, LANES:] = _rope_lanes(q[:, c0 + LANES:c0 + 2 * LANES], cos_t, sin_t).astype(BF16)
        k_ref[h, :, :LANES] = kv[:, c0:c0 + LANES].astype(BF16)
        k_ref[h, :, LANES:] = k_rope
        v_ref[h] = kv[:, c0 + LANES:c0 + 2 * LANES].astype(BF16)


def _mla_prep(proj, col0, g_q, g_kv, wq, wkv, cos_t, sin_t, heads, seq, tm):
    n = proj.shape[0]
    qr, kvr = g_q.shape[0], g_kv.shape[0]
    tm = min(tm, seq)
    ps = seq // tm
    assert col0 % qr == 0 and (col0 + qr) % kvr == 0 and (col0 + qr + kvr) % LANES == 0
    return pl.pallas_call(
        functools.partial(_mla_prep_kernel, heads=heads),
        out_shape=(jax.ShapeDtypeStruct((heads, n, 2 * LANES), BF16),
                   jax.ShapeDtypeStruct((heads, n, 2 * LANES), BF16),
                   jax.ShapeDtypeStruct((heads, n, LANES), BF16)),
        grid=(n // tm,),
        in_specs=[pl.BlockSpec((tm, qr), lambda i: (i, col0 // qr)),
                  pl.BlockSpec((tm, kvr), lambda i: (i, (col0 + qr) // kvr)),
                  pl.BlockSpec((tm, LANES), lambda i: (i, (col0 + qr + kvr) // LANES)),
                  pl.BlockSpec((1, qr), lambda i: (0, 0)),
                  pl.BlockSpec((1, kvr), lambda i: (0, 0)),
                  pl.BlockSpec(wq.shape, lambda i: (0, 0)),
                  pl.BlockSpec(wkv.shape, lambda i: (0, 0)),
                  pl.BlockSpec((tm, LANES), lambda i: (i % ps, 0)),
                  pl.BlockSpec((tm, LANES), lambda i: (i % ps, 0))],
        out_specs=(pl.BlockSpec((heads, tm, 2 * LANES), lambda i: (0, i, 0)),
                   pl.BlockSpec((heads, tm, 2 * LANES), lambda i: (0, i, 0)),
                   pl.BlockSpec((heads, tm, LANES), lambda i: (0, i, 0))),
        compiler_params=_cparams("parallel"),
        name="mla_prep",
    )(proj, proj, proj, g_q.reshape(1, qr), g_kv.reshape(1, kvr), wq, wkv, cos_t, sin_t)


def _online_step(s, vblk, carry):
    m, l, acc = carry
    m_new = jnp.maximum(m, jnp.max(s, axis=1, keepdims=True))
    alpha = jnp.exp(m - m_new)
    p = jnp.exp(s - m_new)
    l = alpha * l + jnp.sum(p, axis=1, keepdims=True)
    acc = alpha * acc + jnp.dot(p.astype(BF16), vblk, preferred_element_type=F32)
    return m_new, l, acc


def _qk(q, k):
    return lax.dot_general(q, k, (((1,), (1,)), ((), ())), preferred_element_type=F32)


def _softmax_init(tq, dv):
    return (jnp.full((tq, 1), -1e30, F32), jnp.zeros((tq, 1), F32), jnp.zeros((tq, dv), F32))


def _mla_flash_kernel(q_ref, k_ref, v_ref, o_ref):
    tq = q_ref.shape[1]
    qi = pl.program_id(2)
    q = q_ref[0]

    def body(j, carry):
        off = pl.multiple_of(j * tq, tq)
        return _online_step(_qk(q, k_ref[0, pl.ds(off, tq), :]), v_ref[0, pl.ds(off, tq), :], carry)

    carry = lax.fori_loop(0, qi, body, _softmax_init(tq, v_ref.shape[2]))
    off = pl.multiple_of(qi * tq, tq)
    causal = (lax.broadcasted_iota(jnp.int32, (tq, tq), 1) <= lax.broadcasted_iota(jnp.int32, (tq, tq), 0))
    s = jnp.where(causal, _qk(q, k_ref[0, pl.ds(off, tq), :]), -jnp.inf)
    _, l, acc = _online_step(s, v_ref[0, pl.ds(off, tq), :], carry)
    o_ref[...] = (acc / l).astype(o_ref.dtype)


def _mla_flash(q, k, v, batch, seq, tq):
    heads, n, dk = q.shape
    dv = v.shape[2]
    tq = min(tq, seq)
    nq = seq // tq
    return pl.pallas_call(
        _mla_flash_kernel,
        out_shape=jax.ShapeDtypeStruct((n, heads * dv), BF16),
        grid=(batch, heads, nq),
        in_specs=[pl.BlockSpec((1, tq, dk), lambda b, h, i: (h, b * nq + i, 0)),
                  pl.BlockSpec((1, seq, dk), lambda b, h, i: (h, b, 0)),
                  pl.BlockSpec((1, seq, dv), lambda b, h, i: (h, b, 0))],
        out_specs=pl.BlockSpec((tq, dv), lambda b, h, i: (b * nq + i, h)),
        compiler_params=_cparams("parallel", "parallel", "arbitrary"),
        name="mla_flash",
    )(q, k, v)


def _diff_flash_kernel(lam_ref, slopes_ref, q_ref, k_ref, v_ref, g_ref, o_ref, *, out_scale):
    tq = q_ref.shape[0]
    dh = q_ref.shape[1] // 2
    dv = v_ref.shape[1]
    qi = pl.program_id(2)
    slope = slopes_ref[0, pl.program_id(1)]
    q1, q2 = q_ref[:, :dh], q_ref[:, dh:]
    col = lax.broadcasted_iota(jnp.int32, (1, tq), 1).astype(F32)

    def scores(off, qpart, lo):
        return _qk(qpart, k_ref[pl.ds(off, tq), lo:lo + dh])

    def body(j, carry):
        c1, c2 = carry
        off = pl.multiple_of(j * tq, tq)
        bias = slope * (col + (off - qi * tq).astype(F32))
        vblk = v_ref[pl.ds(off, tq), :]
        c1 = _online_step(scores(off, q1, 0) + bias, vblk, c1)
        c2 = _online_step(scores(off, q2, dh) + bias, vblk, c2)
        return c1, c2

    c1, c2 = lax.fori_loop(0, qi, body, (_softmax_init(tq, dv), _softmax_init(tq, dv)))
    off = pl.multiple_of(qi * tq, tq)
    causal = (lax.broadcasted_iota(jnp.int32, (tq, tq), 1) <= lax.broadcasted_iota(jnp.int32, (tq, tq), 0))
    bias = slope * col
    vblk = v_ref[pl.ds(off, tq), :]
    _, l1, a1 = _online_step(jnp.where(causal, scores(off, q1, 0) + bias, -jnp.inf), vblk, c1)
    _, l2, a2 = _online_step(jnp.where(causal, scores(off, q2, dh) + bias, -jnp.inf), vblk, c2)
    o = a1 / l1 - lam_ref[0, 0] * (a2 / l2)
    o_ref[...] = (_rms(o, g_ref[...]) * out_scale).astype(o_ref.dtype)


def _diff_flash(qkv, lam, g_subln, batch, seq, heads, out_scale, tq):
    n = qkv.shape[0]
    dv = g_subln.shape[0]
    tq = min(tq, seq)
    nq = seq // tq
    slopes = jnp.asarray((2.0 ** (-8.0 * np.arange(1, heads + 1) / heads)).astype(np.float32)).reshape(1, heads)
    return pl.pallas_call(
        functools.partial(_diff_flash_kernel, out_scale=out_scale),
        out_shape=jax.ShapeDtypeStruct((n, heads * dv), BF16),
        grid=(batch, heads, nq),
        in_specs=[pl.BlockSpec(memory_space=pltpu.SMEM),
                  pl.BlockSpec(memory_space=pltpu.SMEM),
                  pl.BlockSpec((tq, dv), lambda b, h, i: (b * nq + i, h)),
                  pl.BlockSpec((seq, dv), lambda b, h, i: (b, heads + h)),
                  pl.BlockSpec((seq, dv), lambda b, h, i: (b, 2 * heads + h)),
                  pl.BlockSpec((1, dv), lambda b, h, i: (0, 0))],
        out_specs=pl.BlockSpec((tq, dv), lambda b, h, i: (b * nq + i, h)),
        compiler_params=_cparams("parallel", "parallel", "arbitrary"),
        name="diff_flash",
    )(lam.reshape(1, 1), slopes, qkv, qkv, qkv, g_subln.reshape(1, dv))


def _router_kernel(x_ref, g_ref, w_ref, b_ref, h_ref, ids_ref, wts_ref, *, n_groups, n_experts):
    h = _rms(x_ref[...], g_ref[...])
    h_ref[...] = h
    h_hi = h.astype(BF16)
    h_lo = (h - h_hi.astype(F32)).astype(BF16)
    hi_both = jnp.dot(h_hi, w_ref[...], preferred_element_type=F32)
    lo_hi = jnp.dot(h_lo, w_ref[:, :LANES], preferred_element_type=F32)
    logits = hi_both[:, :LANES] + hi_both[:, LANES:] + lo_hi + b_ref[...]
    lane = lax.broadcasted_iota(jnp.int32, logits.shape, 1).astype(F32)
    neg = -jnp.inf
    per_group = n_experts // n_groups
    gl = jnp.where(lane < n_groups, logits, neg)
    gmax = jnp.max(gl, axis=1, keepdims=True)
    g_prob = 1.0 / jnp.sum(jnp.exp(gl - gmax), axis=1, keepdims=True)
    g_idx = jnp.min(jnp.where(gl == gmax, lane, float(LANES)), axis=1, keepdims=True)
    lo = n_groups + g_idx * per_group
    el = jnp.where((lane >= lo) & (lane < lo + per_group), logits, neg)
    m1 = jnp.max(el, axis=1, keepdims=True)
    i1 = jnp.min(jnp.where(el == m1, lane, float(LANES)), axis=1, keepdims=True)
    el2 = jnp.where(lane == i1, neg, el)
    m2 = jnp.max(el2, axis=1, keepdims=True)
    i2 = jnp.min(jnp.where(el2 == m2, lane, float(LANES)), axis=1, keepdims=True)
    e2 = jnp.exp(m2 - m1)
    w1 = g_prob / (1.0 + e2)
    w2 = g_prob * e2 / (1.0 + e2)
    ids = jnp.where(lane == 0.0, i1 - n_groups, jnp.where(lane == 1.0, i2 - n_groups, 0.0))
    ids_ref[...] = ids.astype(jnp.int32)
    wts_ref[...] = jnp.where(lane == 0.0, w1, jnp.where(lane == 1.0, w2, 0.0))


def _router(x, g, w_r, b_r, n_groups, n_experts, tm):
    n, d = x.shape
    tm = min(tm, n)
    return pl.pallas_call(
        functools.partial(_router_kernel, n_groups=n_groups, n_experts=n_experts),
        out_shape=(jax.ShapeDtypeStruct((n, d), F32),
                   jax.ShapeDtypeStruct((n, LANES), jnp.int32),
                   jax.ShapeDtypeStruct((n, LANES), F32)),
        grid=(n // tm,),
        in_specs=[pl.BlockSpec((tm, d), lambda i: (i, 0)),
                  pl.BlockSpec((1, d), lambda i: (0, 0)),
                  pl.BlockSpec((d, 2 * LANES), lambda i: (0, 0)),
                  pl.BlockSpec((1, LANES), lambda i: (0, 0))],
        out_specs=(pl.BlockSpec((tm, d), lambda i: (i, 0)),
                   pl.BlockSpec((tm, LANES), lambda i: (i, 0)),
                   pl.BlockSpec((tm, LANES), lambda i: (i, 0))),
        compiler_params=_cparams("parallel"),
        name="router",
    )(x, g.reshape(1, d), w_r, b_r)


def _moe_plan(ids, n_experts, tile):
    n = ids.shape[0]
    na = TOP_K * n
    n_tiles = na // tile + n_experts
    ef = ids[:, :TOP_K].T.reshape(na)
    onehot = (ef[:, None] == jnp.arange(n_experts, dtype=jnp.int32)[None, :]).astype(jnp.int32)
    csum = jnp.cumsum(onehot, axis=0)
    rank = jnp.sum(csum * onehot, axis=1) - 1
    counts = csum[-1]
    tiles_per = (counts + tile - 1) // tile
    tile_end = jnp.cumsum(tiles_per)
    tile_start = tile_end - tiles_per
    pos = (tile_start[ef] * tile + rank).astype(jnp.int32)
    a = jnp.arange(na, dtype=jnp.int32)
    src_tok = jnp.zeros((n_tiles * tile,), jnp.int32).at[pos].set(a % n)
    t = jnp.arange(n_tiles, dtype=jnp.int32)
    used = tile_end[-1]
    tile_expert = jnp.sum((tile_end[None, :] <= jnp.minimum(t, used - 1)[:, None]).astype(jnp.int32), axis=1)
    live = (t < used).astype(jnp.int32)
    return tile_expert, live, src_tok, pos


def _expert_kernel(te_ref, live_ref, src_ref, h_hbm, wug_ref, wd_ref, o_ref, xbuf, wug_bf, wd_bf, gsem, *, tile, ff):
    i = pl.program_id(0)
    slot = i % 2
    prev = jnp.maximum(i - 1, 0)

    def start_gather(t, s):
        for static_s in range(2):
            @pl.when(s == static_s)
            def _():
                for r in range(tile):
                    pltpu.make_async_copy(h_hbm.at[pl.ds(src_ref[t * tile + r], 1)],
                                          xbuf.at[static_s, pl.ds(r, 1)], gsem.at[static_s]).start()

    def wait_gather(s):
        pltpu.make_async_copy(h_hbm.at[pl.ds(0, tile)], xbuf.at[s], gsem.at[s]).wait()

    @pl.when(i == 0)
    def _():
        start_gather(0, 0)

    @pl.when(live_ref[i] > 0)
    def _():
        @pl.when((i == 0) | (te_ref[i] != te_ref[prev]))
        def _():
            for r0 in range(0, wug_ref.shape[1], CAST_ROWS):
                wug_bf[r0:r0 + CAST_ROWS, :] = wug_ref[0, r0:r0 + CAST_ROWS, :].astype(BF16)
            for r0 in range(0, wd_ref.shape[1], CAST_ROWS):
                wd_bf[r0:r0 + CAST_ROWS, :] = wd_ref[0, r0:r0 + CAST_ROWS, :].astype(BF16)

        wait_gather(slot)
        start_gather(i + 1, 1 - slot)
        gu = jnp.dot(xbuf[slot].astype(BF16), wug_bf[...], preferred_element_type=F32)
        gate, up = gu[:, :ff], gu[:, ff:]
        act = (gate * jax.nn.sigmoid(gate) * up).astype(BF16)
        o_ref[...] = jnp.dot(act, wd_bf[...], preferred_element_type=F32)

    @pl.when(live_ref[i] == 0)
    def _():
        @pl.when((i > 0) & (live_ref[prev] > 0))
        def _():
            wait_gather(slot)
        o_ref[...] = jnp.zeros_like(o_ref)


def _experts(h, plan, wug, wd, tile):
    tile_expert, live, src_tok, _ = plan
    d, ff = wd.shape[2], wd.shape[1]
    n_tiles = tile_expert.shape[0]
    grid_spec = pltpu.PrefetchScalarGridSpec(
        num_scalar_prefetch=3,
        grid=(n_tiles,),
        in_specs=[pl.BlockSpec(memory_space=pl.ANY),
                  pl.BlockSpec((1, d, 2 * ff), lambda i, te, lv, s: (te[i], 0, 0)),
                  pl.BlockSpec((1, ff, d), lambda i, te, lv, s: (te[i], 0, 0))],
        out_specs=pl.BlockSpec((tile, d), lambda i, te, lv, s: (i, 0)),
        scratch_shapes=[pltpu.VMEM((2, tile, d), F32),
                        pltpu.VMEM((d, 2 * ff), BF16), pltpu.VMEM((ff, d), BF16),
                        pltpu.SemaphoreType.DMA((2,))],
    )
    return pl.pallas_call(
        functools.partial(_expert_kernel, tile=tile, ff=ff),
        out_shape=jax.ShapeDtypeStruct((n_tiles * tile, d), F32),
        grid_spec=grid_spec,
        compiler_params=_cparams("arbitrary"),
        name="experts",
    )(tile_expert, live, src_tok, h, wug, wd)


def _combine_kernel(pos_ref, x_ref, wts_ref, g_ref, y_hbm, *rest, emit_x, n_tokens):
    *out_refs, ybuf, sem = rest
    tm = x_ref.shape[0]
    i = pl.program_id(0)
    slot = i % 2

    def start_gather(t, s):
        for static_s in range(2):
            @pl.when(s == static_s)
            def _():
                for r in range(tm):
                    for k in range(TOP_K):
                        pltpu.make_async_copy(y_hbm.at[pl.ds(pos_ref[k * n_tokens + t * tm + r], 1)],
                                              ybuf.at[static_s, k, pl.ds(r, 1)], sem.at[static_s]).start()

    @pl.when(i == 0)
    def _():
        start_gather(0, 0)

    for k in range(TOP_K):
        pltpu.make_async_copy(y_hbm.at[pl.ds(0, tm)], ybuf.at[slot, k], sem.at[slot]).wait()

    @pl.when(i + 1 < pl.num_programs(0))
    def _():
        start_gather(i + 1, 1 - slot)

    wts = wts_ref[...]
    x_new = x_ref[...] + wts[:, 0:1] * ybuf[slot, 0] + wts[:, 1:2] * ybuf[slot, 1]
    if emit_x:
        out_refs[0][...] = x_new
    out_refs[-1][...] = _rms(x_new, g_ref[...]).astype(out_refs[-1].dtype)


def _combine(x, y_sorted, pos, wts, g, emit_x, norm_dtype, tm):
    n, d = x.shape
    tm = min(tm, n)
    row = pl.BlockSpec((tm, d), lambda i, p: (i, 0))
    out_shape = [jax.ShapeDtypeStruct((n, d), norm_dtype)]
    out_specs = [row]
    if emit_x:
        out_shape.insert(0, jax.ShapeDtypeStruct((n, d), F32))
        out_specs.insert(0, row)
    grid_spec = pltpu.PrefetchScalarGridSpec(
        num_scalar_prefetch=1,
        grid=(n // tm,),
        in_specs=[row, pl.BlockSpec((tm, LANES), lambda i, p: (i, 0)), pl.BlockSpec((1, d), lambda i, p: (0, 0)),
                  pl.BlockSpec(memory_space=pl.ANY)],
        out_specs=tuple(out_specs),
        scratch_shapes=[pltpu.VMEM((2, TOP_K, tm, d), F32), pltpu.SemaphoreType.DMA((2,))],
    )
    return pl.pallas_call(
        functools.partial(_combine_kernel, emit_x=emit_x, n_tokens=n),
        out_shape=tuple(out_shape),
        grid_spec=grid_spec,
        compiler_params=_cparams("arbitrary"),
        name="combine",
    )(pos, x, wts, g.reshape(1, d), y_sorted)


def _hier_moe(x_mid, g_ffn, w_group, b_group, w_expert, b_expert, w_up_gate, w_down, g_next, emit_x, norm_dtype):
    n, d = x_mid.shape
    n_experts = w_expert.shape[1]
    pad = LANES - N_GROUPS - n_experts
    w_r = jnp.concatenate([w_group, w_expert, jnp.zeros((d, pad), F32)], axis=1)
    b_r = jnp.concatenate([b_group, b_expert, jnp.zeros((pad,), F32)]).reshape(1, LANES)
    w_hi = w_r.astype(BF16)
    w_lo = (w_r - w_hi.astype(F32)).astype(BF16)
    h, ids, wts = _router(x_mid, g_ffn, jnp.concatenate([w_hi, w_lo], axis=1), b_r, N_GROUPS, n_experts, tm=256)
    plan = _moe_plan(ids, n_experts, MOE_TILE)
    y_sorted = _experts(h, plan, w_up_gate, w_down, MOE_TILE)
    return _combine(x_mid, y_sorted, plan[3], wts, g_next, emit_x, norm_dtype, tm=256)


def _rope_tables(seq, rope_dim):
    pos = jnp.arange(seq, dtype=F32)
    inv_freq = ROPE_THETA ** (-jnp.arange(0, rope_dim, 2, dtype=F32) / rope_dim)
    ang = pos[:, None] * inv_freq[None, :]
    cos, sin = jnp.cos(ang), jnp.sin(ang)
    zero = jnp.zeros_like(cos)
    return (jnp.concatenate([cos, cos, zero, zero], axis=1),
            jnp.concatenate([-sin, sin, zero, zero], axis=1))


def kernel(x, l0_g_mix, l0_w_in, l0_conv_w, l0_g_q_lat, l0_w_uq, l0_g_kv_lat, l0_w_ukv, l0_w_out, l0_g_ffn, l0_w_group, l0_b_group, l0_w_expert, l0_b_expert, l0_w_up_gate, l0_w_down, l1_g_mix, l1_w_qkv, l1_lam_q1, l1_lam_k1, l1_lam_q2, l1_lam_k2, l1_g_subln, l1_w_out, l1_g_ffn, l1_w_group, l1_b_group, l1_w_expert, l1_b_expert, l1_w_up_gate, l1_w_down, g_final):
    batch, seq, d = x.shape
    n = batch * seq
    x2 = x.reshape(n, d)
    cw = l0_conv_w.shape[1]
    qr, kvr = l0_g_q_lat.shape[0], l0_g_kv_lat.shape[0]
    heads = (d - cw) // LANES
    rope = l0_w_in.shape[1] - 3 * cw - qr - kvr
    nope = l0_w_uq.shape[1] // heads - rope
    assert rope == LANES // 2 and nope == LANES

    half = rope // 2
    kr0 = 3 * cw + qr + kvr
    t1, t2 = l0_w_in[:, kr0:kr0 + half], l0_w_in[:, kr0 + half:kr0 + rope]
    used_cols = kr0 + 2 * rope
    in_width = -(-used_cols // 512) * 512
    w_in = jnp.concatenate([l0_w_in[:, :kr0], t1, t2, t2, t1, jnp.zeros((d, in_width - used_cols), F32)], axis=1)
    proj = _norm_matmul(x2, l0_g_mix, w_in.astype(BF16), F32, tm=1024, tn=512)
    y_conv = _gated_conv(proj, l0_conv_w, seq, tm=512)

    scale_mla = (nope + rope) ** -0.5
    wq = l0_w_uq.reshape(qr, heads, nope + rope) * scale_mla
    wq = jnp.concatenate([wq[..., :nope], wq[..., nope:nope + half], wq[..., nope + half:],
                          wq[..., nope + half:], wq[..., nope:nope + half]], axis=-1)
    wq = wq.reshape(qr, heads * 2 * LANES).astype(BF16)
    cos_t, sin_t = _rope_tables(seq, rope)
    q, k, v = _mla_prep(proj, 3 * cw, l0_g_q_lat, l0_g_kv_lat, wq, l0_w_ukv.astype(BF16), cos_t, sin_t,
                        heads, seq, tm=256)
    y_mla = _mla_flash(q, k, v, batch, seq, tq=1024)
    x_mid = _outproj(y_conv, y_mla, 0, l0_w_out.astype(BF16), x2, tm=512)
    x1, h1 = _hier_moe(x_mid, l0_g_ffn, l0_w_group, l0_b_group, l0_w_expert, l0_b_expert,
                       l0_w_up_gate, l0_w_down, l1_g_mix, True, BF16)

    dv = l1_g_subln.shape[0]
    dheads = d // dv
    dh = dv // 2
    lam_init = 0.8 - 0.6 * math.exp(-0.3 * 1)
    lam = (jnp.exp(jnp.sum(l1_lam_q1 * l1_lam_k1)) - jnp.exp(jnp.sum(l1_lam_q2 * l1_lam_k2)) + lam_init)
    w_qkv = jnp.concatenate([l1_w_qkv[:, :d] * dh ** -0.5, l1_w_qkv[:, d:]], axis=1).astype(BF16)
    qkv = _matmul(h1, w_qkv, BF16, tm=1024, tn=512)
    o = _diff_flash(qkv, lam, l1_g_subln, batch, seq, dheads, 1.0 - lam_init, tq=1024)
    x_mid = _outproj(o, o, 1, l1_w_out.astype(BF16), x1, tm=512)
    (out,) = _hier_moe(x_mid, l1_g_ffn, l1_w_group, l1_b_group, l1_w_expert, l1_b_expert,
                       l1_w_up_gate, l1_w_down, g_final, False, F32)
    return out.reshape(batch, seq, d)
```

```python
import functools
import math

import jax
import jax.numpy as jnp
import numpy as np
from jax import lax
from jax.experimental import pallas as pl
from jax.experimental.pallas import tpu as pltpu

F32 = jnp.float32
BF16 = jnp.bfloat16

NORM_EPS = 1e-6
ROPE_THETA = 10000.0
LANES = 128
SUBLANES = 8
MOE_TILE = 256
NORM_ROWS = 256
CAST_ROWS = 256
N_GROUPS = 4
TOP_K = 2
VMEM_LIMIT_BYTES = 52 * 1024 * 1024


def _cparams(*sem):
    return pltpu.CompilerParams(dimension_semantics=sem, vmem_limit_bytes=VMEM_LIMIT_BYTES)


def _rms(xf, g):
    ms = jnp.mean(xf * xf, axis=-1, keepdims=True)
    return xf * lax.rsqrt(ms + NORM_EPS) * g


def _norm_matmul_kernel(x_ref, g_ref, w_ref, o_ref, h_ref):
    @pl.when(pl.program_id(1) == 0)
    def _():
        rows = min(NORM_ROWS, x_ref.shape[0])
        for r0 in range(0, x_ref.shape[0], rows):
            h_ref[r0:r0 + rows, :] = _rms(x_ref[r0:r0 + rows, :], g_ref[...]).astype(BF16)

    o_ref[...] = jnp.dot(h_ref[...], w_ref[...], preferred_element_type=F32).astype(o_ref.dtype)


def _norm_matmul(x, g, w, out_dtype, tm, tn):
    n, k = x.shape
    nn = w.shape[1]
    tm, tn = min(tm, n), min(tn, nn)
    return pl.pallas_call(
        _norm_matmul_kernel,
        out_shape=jax.ShapeDtypeStruct((n, nn), out_dtype),
        grid=(n // tm, nn // tn),
        in_specs=[pl.BlockSpec((tm, k), lambda i, j: (i, 0)),
                  pl.BlockSpec((1, k), lambda i, j: (0, 0)),
                  pl.BlockSpec((k, tn), lambda i, j: (0, j))],
        out_specs=pl.BlockSpec((tm, tn), lambda i, j: (i, j)),
        scratch_shapes=[pltpu.VMEM((tm, k), BF16)],
        compiler_params=_cparams("parallel", "arbitrary"),
        name="norm_matmul",
    )(x, g.reshape(1, k), w)


def _matmul_kernel(x_ref, w_ref, o_ref):
    o_ref[...] = jnp.dot(x_ref[...], w_ref[...], preferred_element_type=F32).astype(o_ref.dtype)


def _matmul(x, w, out_dtype, tm, tn):
    n, k = x.shape
    nn = w.shape[1]
    tm, tn = min(tm, n), min(tn, nn)
    return pl.pallas_call(
        _matmul_kernel,
        out_shape=jax.ShapeDtypeStruct((n, nn), out_dtype),
        grid=(n // tm, nn // tn),
        in_specs=[pl.BlockSpec((tm, k), lambda i, j: (i, 0)),
                  pl.BlockSpec((k, tn), lambda i, j: (0, j))],
        out_specs=pl.BlockSpec((tm, tn), lambda i, j: (i, j)),
        compiler_params=_cparams("parallel", "arbitrary"),
        name="matmul",
    )(x, w)


def _outproj_kernel(ya_ref, yb_ref, w_ref, x_ref, g_ref, wr_ref, br_ref, o_ref, h_ref, ids_ref, wts_ref,
                    *, n_groups, n_experts):
    ka = ya_ref.shape[1]
    acc = jnp.dot(ya_ref[...], w_ref[:ka, :], preferred_element_type=F32)
    acc = acc + jnp.dot(yb_ref[...], w_ref[ka:, :], preferred_element_type=F32)
    x_mid = x_ref[...] + acc
    o_ref[...] = x_mid
    _route(x_mid, g_ref, wr_ref, br_ref, h_ref, ids_ref, wts_ref, n_groups, n_experts)


def _outproj(ya, yb, yb_col, w, x, g_ffn, w_r, b_r, n_experts, tm):
    n, d = x.shape
    ka = d // 2
    tm = min(tm, n)
    row = pl.BlockSpec((tm, d), lambda i: (i, 0))
    lane_row = pl.BlockSpec((tm, LANES), lambda i: (i, 0))
    return pl.pallas_call(
        functools.partial(_outproj_kernel, n_groups=N_GROUPS, n_experts=n_experts),
        out_shape=(jax.ShapeDtypeStruct((n, d), F32), jax.ShapeDtypeStruct((n, d), F32),
                   jax.ShapeDtypeStruct((n, LANES), jnp.int32), jax.ShapeDtypeStruct((n, LANES), F32)),
        grid=(n // tm,),
        in_specs=[pl.BlockSpec((tm, ka), lambda i: (i, 0)),
                  pl.BlockSpec((tm, ka), lambda i: (i, yb_col)),
                  pl.BlockSpec((d, d), lambda i: (0, 0)),
                  row,
                  pl.BlockSpec((1, d), lambda i: (0, 0)),
                  pl.BlockSpec((d, 2 * LANES), lambda i: (0, 0)),
                  pl.BlockSpec((1, LANES), lambda i: (0, 0))],
        out_specs=(row, row, lane_row, lane_row),
        compiler_params=_cparams("parallel"),
        name="outproj_router",
    )(ya, yb, w, x, g_ffn.reshape(1, d), w_r, b_r)


def _conv_kernel(b_ref, c_ref, u_ref, hc_ref, hu_ref, w_ref, o_ref, *, tiles_per_seq):
    tm = c_ref.shape[0]
    first = (pl.program_id(0) % tiles_per_seq) == 0
    v = c_ref[...] * u_ref[...]
    halo = jnp.where(first, 0.0, hc_ref[...] * hu_ref[...])
    row = lax.broadcasted_iota(jnp.int32, v.shape, 0)
    v1 = jnp.where(row == 0, halo[7:8, :], pltpu.roll(v, 1, axis=0))
    v2 = jnp.where(row == 0, halo[6:7, :],
                   jnp.where(row == 1, halo[7:8, :], pltpu.roll(v, 2, axis=0)))
    w = w_ref[...]
    conv = w[2:3, :] * v + w[1:2, :] * v1 + w[0:1, :] * v2
    o_ref[...] = (b_ref[...] * conv).astype(o_ref.dtype)


def _gated_conv(proj, conv_w, seq, tm):
    n = proj.shape[0]
    cw = conv_w.shape[1]
    tm = min(tm, seq)
    hb = tm // 8
    return pl.pallas_call(
        functools.partial(_conv_kernel, tiles_per_seq=seq // tm),
        out_shape=jax.ShapeDtypeStruct((n, cw), BF16),
        grid=(n // tm,),
        in_specs=[pl.BlockSpec((tm, cw), lambda i: (i, 0)),
                  pl.BlockSpec((tm, cw), lambda i: (i, 1)),
                  pl.BlockSpec((tm, cw), lambda i: (i, 2)),
                  pl.BlockSpec((8, cw), lambda i: (jnp.maximum(i * hb - 1, 0), 1)),
                  pl.BlockSpec((8, cw), lambda i: (jnp.maximum(i * hb - 1, 0), 2)),
                  pl.BlockSpec((conv_w.shape[0], cw), lambda i: (0, 0))],
        out_specs=pl.BlockSpec((tm, cw), lambda i: (i, 0)),
        compiler_params=_cparams("parallel"),
        name="gated_conv",
    )(proj, proj, proj, proj, proj, conv_w)


def _rope_lanes(a, cos_t, sin_t):
    return a * cos_t + pltpu.roll(a, LANES // 2, axis=1) * sin_t


def _mla_prep_kernel(cq_ref, ckv_ref, kr_ref, gq_ref, gkv_ref, wq_ref, wkv_ref, cos_ref, sin_ref,
                     q_ref, k_ref, v_ref, *, heads):
    cqn = _rms(cq_ref[...], gq_ref[...]).astype(BF16)
    ckvn = _rms(ckv_ref[...], gkv_ref[...]).astype(BF16)
    cos_t, sin_t = cos_ref[...], sin_ref[...]
    q = jnp.dot(cqn, wq_ref[...], preferred_element_type=F32)
    kv = jnp.dot(ckvn, wkv_ref[...], preferred_element_type=F32)
    k_rope = _rope_lanes(kr_ref[...], cos_t, sin_t).astype(BF16)
    for h in range(heads):
        c0 = h * 2 * LANES
        q_ref[h, :, :LANES] = q[:, c0:c0 + LANES].astype(BF16)
        q_ref[h, :, LANES:] = _rope_lanes(q[:, c0 + LANES:c0 + 2 * LANES], cos_t, sin_t).astype(BF16)
        k_ref[h, :, :LANES] = kv[:, c0:c0 + LANES].astype(BF16)
        k_ref[h, :, LANES:] = k_rope
        v_ref[h] = kv[:, c0 + LANES:c0 + 2 * LANES].astype(BF16)


def _mla_prep(proj, col0, g_q, g_kv, wq, wkv, cos_t, sin_t, heads, seq, tm):
    n = proj.shape[0]
    qr, kvr = g_q.shape[0], g_kv.shape[0]
    tm = min(tm, seq)
    ps = seq // tm
    assert col0 % qr == 0 and (col0 + qr) % kvr == 0 and (col0 + qr + kvr) % LANES == 0
    return pl.pallas_call(
        functools.partial(_mla_prep_kernel, heads=heads),
        out_shape=(jax.ShapeDtypeStruct((heads, n, 2 * LANES), BF16),
                   jax.ShapeDtypeStruct((heads, n, 2 * LANES), BF16),
                   jax.ShapeDtypeStruct((heads, n, LANES), BF16)),
        grid=(n // tm,),
        in_specs=[pl.BlockSpec((tm, qr), lambda i: (i, col0 // qr)),
                  pl.BlockSpec((tm, kvr), lambda i: (i, (col0 + qr) // kvr)),
                  pl.BlockSpec((tm, LANES), lambda i: (i, (col0 + qr + kvr) // LANES)),
                  pl.BlockSpec((1, qr), lambda i: (0, 0)),
                  pl.BlockSpec((1, kvr), lambda i: (0, 0)),
                  pl.BlockSpec(wq.shape, lambda i: (0, 0)),
                  pl.BlockSpec(wkv.shape, lambda i: (0, 0)),
                  pl.BlockSpec((tm, LANES), lambda i: (i % ps, 0)),
                  pl.BlockSpec((tm, LANES), lambda i: (i % ps, 0))],
        out_specs=(pl.BlockSpec((heads, tm, 2 * LANES), lambda i: (0, i, 0)),
                   pl.BlockSpec((heads, tm, 2 * LANES), lambda i: (0, i, 0)),
                   pl.BlockSpec((heads, tm, LANES), lambda i: (0, i, 0))),
        compiler_params=_cparams("parallel"),
        name="mla_prep",
    )(proj, proj, proj, g_q.reshape(1, qr), g_kv.reshape(1, kvr), wq, wkv, cos_t, sin_t)


def _online_step(s, vblk, carry):
    m, l, acc = carry
    m_new = jnp.maximum(m, jnp.max(s, axis=1, keepdims=True))
    alpha = jnp.exp(m - m_new)
    p = jnp.exp(s - m_new)
    l = alpha * l + jnp.sum(p, axis=1, keepdims=True)
    acc = alpha * acc + jnp.dot(p.astype(BF16), vblk, preferred_element_type=F32)
    return m_new, l, acc


def _qk(q, k):
    return lax.dot_general(q, k, (((1,), (1,)), ((), ())), preferred_element_type=F32)


def _softmax_init(tq, dv):
    return (jnp.full((tq, 1), -1e30, F32), jnp.zeros((tq, 1), F32), jnp.zeros((tq, dv), F32))


def _mla_flash_kernel(q_ref, k_ref, v_ref, o_ref):
    tq = q_ref.shape[1]
    qi = pl.program_id(2)
    q = q_ref[0]

    def body(j, carry):
        off = pl.multiple_of(j * tq, tq)
        return _online_step(_qk(q, k_ref[0, pl.ds(off, tq), :]), v_ref[0, pl.ds(off, tq), :], carry)

    carry = lax.fori_loop(0, qi, body, _softmax_init(tq, v_ref.shape[2]))
    off = pl.multiple_of(qi * tq, tq)
    causal = (lax.broadcasted_iota(jnp.int32, (tq, tq), 1) <= lax.broadcasted_iota(jnp.int32, (tq, tq), 0))
    s = jnp.where(causal, _qk(q, k_ref[0, pl.ds(off, tq), :]), -jnp.inf)
    _, l, acc = _online_step(s, v_ref[0, pl.ds(off, tq), :], carry)
    o_ref[...] = (acc / l).astype(o_ref.dtype)


def _mla_flash(q, k, v, batch, seq, tq):
    heads, n, dk = q.shape
    dv = v.shape[2]
    tq = min(tq, seq)
    nq = seq // tq
    return pl.pallas_call(
        _mla_flash_kernel,
        out_shape=jax.ShapeDtypeStruct((n, heads * dv), BF16),
        grid=(batch, heads, nq),
        in_specs=[pl.BlockSpec((1, tq, dk), lambda b, h, i: (h, b * nq + i, 0)),
                  pl.BlockSpec((1, seq, dk), lambda b, h, i: (h, b, 0)),
                  pl.BlockSpec((1, seq, dv), lambda b, h, i: (h, b, 0))],
        out_specs=pl.BlockSpec((tq, dv), lambda b, h, i: (b * nq + i, h)),
        compiler_params=_cparams("parallel", "parallel", "arbitrary"),
        name="mla_flash",
    )(q, k, v)


def _diff_flash_kernel(lam_ref, slopes_ref, q_ref, k_ref, v_ref, g_ref, o_ref, *, out_scale):
    tq = q_ref.shape[0]
    dh = q_ref.shape[1] // 2
    dv = v_ref.shape[1]
    qi = pl.program_id(2)
    slope = slopes_ref[0, pl.program_id(1)]
    q1, q2 = q_ref[:, :dh], q_ref[:, dh:]
    col = lax.broadcasted_iota(jnp.int32, (1, tq), 1).astype(F32)

    def scores(off, qpart, lo):
        return _qk(qpart, k_ref[pl.ds(off, tq), lo:lo + dh])

    def body(j, carry):
        c1, c2 = carry
        off = pl.multiple_of(j * tq, tq)
        bias = slope * (col + (off - qi * tq).astype(F32))
        vblk = v_ref[pl.ds(off, tq), :]
        c1 = _online_step(scores(off, q1, 0) + bias, vblk, c1)
        c2 = _online_step(scores(off, q2, dh) + bias, vblk, c2)
        return c1, c2

    c1, c2 = lax.fori_loop(0, qi, body, (_softmax_init(tq, dv), _softmax_init(tq, dv)))
    off = pl.multiple_of(qi * tq, tq)
    causal = (lax.broadcasted_iota(jnp.int32, (tq, tq), 1) <= lax.broadcasted_iota(jnp.int32, (tq, tq), 0))
    bias = slope * col
    vblk = v_ref[pl.ds(off, tq), :]
    _, l1, a1 = _online_step(jnp.where(causal, scores(off, q1, 0) + bias, -jnp.inf), vblk, c1)
    _, l2, a2 = _online_step(jnp.where(causal, scores(off, q2, dh) + bias, -jnp.inf), vblk, c2)
    o = a1 / l1 - lam_ref[0, 0] * (a2 / l2)
    o_ref[...] = (_rms(o, g_ref[...]) * out_scale).astype(o_ref.dtype)


def _diff_flash(qkv, lam, g_subln, batch, seq, heads, out_scale, tq):
    n = qkv.shape[0]
    dv = g_subln.shape[0]
    tq = min(tq, seq)
    nq = seq // tq
    slopes = jnp.asarray((2.0 ** (-8.0 * np.arange(1, heads + 1) / heads)).astype(np.float32)).reshape(1, heads)
    return pl.pallas_call(
        functools.partial(_diff_flash_kernel, out_scale=out_scale),
        out_shape=jax.ShapeDtypeStruct((n, heads * dv), BF16),
        grid=(batch, heads, nq),
        in_specs=[pl.BlockSpec(memory_space=pltpu.SMEM),
                  pl.BlockSpec(memory_space=pltpu.SMEM),
                  pl.BlockSpec((tq, dv), lambda b, h, i: (b * nq + i, h)),
                  pl.BlockSpec((seq, dv), lambda b, h, i: (b, heads + h)),
                  pl.BlockSpec((seq, dv), lambda b, h, i: (b, 2 * heads + h)),
                  pl.BlockSpec((1, dv), lambda b, h, i: (0, 0))],
        out_specs=pl.BlockSpec((tq, dv), lambda b, h, i: (b * nq + i, h)),
        compiler_params=_cparams("parallel", "parallel", "arbitrary"),
        name="diff_flash",
    )(lam.reshape(1, 1), slopes, qkv, qkv, qkv, g_subln.reshape(1, dv))


def _route(x_mid, g_ref, w_ref, b_ref, h_ref, ids_ref, wts_ref, n_groups, n_experts):
    h = _rms(x_mid, g_ref[...])
    h_ref[...] = h
    h_hi = h.astype(BF16)
    h_lo = (h - h_hi.astype(F32)).astype(BF16)
    hi_both = jnp.dot(h_hi, w_ref[...], preferred_element_type=F32)
    lo_hi = jnp.dot(h_lo, w_ref[:, :LANES], preferred_element_type=F32)
    logits = hi_both[:, :LANES] + hi_both[:, LANES:] + lo_hi + b_ref[...]
    lane = lax.broadcasted_iota(jnp.int32, logits.shape, 1).astype(F32)
    neg = -jnp.inf
    per_group = n_experts // n_groups
    gl = jnp.where(lane < n_groups, logits, neg)
    gmax = jnp.max(gl, axis=1, keepdims=True)
    g_prob = 1.0 / jnp.sum(jnp.exp(gl - gmax), axis=1, keepdims=True)
    g_idx = jnp.min(jnp.where(gl == gmax, lane, float(LANES)), axis=1, keepdims=True)
    lo = n_groups + g_idx * per_group
    el = jnp.where((lane >= lo) & (lane < lo + per_group), logits, neg)
    m1 = jnp.max(el, axis=1, keepdims=True)
    i1 = jnp.min(jnp.where(el == m1, lane, float(LANES)), axis=1, keepdims=True)
    el2 = jnp.where(lane == i1, neg, el)
    m2 = jnp.max(el2, axis=1, keepdims=True)
    i2 = jnp.min(jnp.where(el2 == m2, lane, float(LANES)), axis=1, keepdims=True)
    e2 = jnp.exp(m2 - m1)
    w1 = g_prob / (1.0 + e2)
    w2 = g_prob * e2 / (1.0 + e2)
    ids = jnp.where(lane == 0.0, i1 - n_groups, jnp.where(lane == 1.0, i2 - n_groups, 0.0))
    ids_ref[...] = ids.astype(jnp.int32)
    wts_ref[...] = jnp.where(lane == 0.0, w1, jnp.where(lane == 1.0, w2, 0.0))


def _router_params(w_group, b_group, w_expert, b_expert):
    d, n_experts = w_expert.shape
    pad = LANES - N_GROUPS - n_experts
    w_r = jnp.concatenate([w_group, w_expert, jnp.zeros((d, pad), F32)], axis=1)
    b_r = jnp.concatenate([b_group, b_expert, jnp.zeros((pad,), F32)]).reshape(1, LANES)
    w_hi = w_r.astype(BF16)
    w_lo = (w_r - w_hi.astype(F32)).astype(BF16)
    return jnp.concatenate([w_hi, w_lo], axis=1), b_r


def _moe_plan(ids, n_experts, tile):
    n = ids.shape[0]
    na = TOP_K * n
    n_tiles = na // tile + n_experts
    ef = ids[:, :TOP_K].T.reshape(na)
    onehot = (ef[:, None] == jnp.arange(n_experts, dtype=jnp.int32)[None, :]).astype(jnp.int32)
    csum = jnp.cumsum(onehot, axis=0)
    rank = jnp.sum(csum * onehot, axis=1) - 1
    counts = csum[-1]
    tiles_per = (counts + tile - 1) // tile
    tile_end = jnp.cumsum(tiles_per)
    tile_start = tile_end - tiles_per
    pos = (tile_start[ef] * tile + rank).astype(jnp.int32)
    a = jnp.arange(na, dtype=jnp.int32)
    src_tok = jnp.zeros((n_tiles * tile,), jnp.int32).at[pos].set(a % n)
    t = jnp.arange(n_tiles, dtype=jnp.int32)
    used = tile_end[-1]
    tile_expert = jnp.sum((tile_end[None, :] <= jnp.minimum(t, used - 1)[:, None]).astype(jnp.int32), axis=1)
    live = (t < used).astype(jnp.int32)
    return tile_expert, live, src_tok, pos


def _expert_kernel(te_ref, live_ref, src_ref, h_hbm, wug_ref, wd_ref, o_ref, xbuf, wug_bf, wd_bf, gsem, *, tile, ff):
    i = pl.program_id(0)
    slot = i % 2
    prev = jnp.maximum(i - 1, 0)

    def start_gather(t, s):
        for static_s in range(2):
            @pl.when(s == static_s)
            def _():
                for r in range(tile):
                    pltpu.make_async_copy(h_hbm.at[pl.ds(src_ref[t * tile + r], 1)],
                                          xbuf.at[static_s, pl.ds(r, 1)], gsem.at[static_s]).start()

    def wait_gather(s):
        pltpu.make_async_copy(h_hbm.at[pl.ds(0, tile)], xbuf.at[s], gsem.at[s]).wait()

    @pl.when(i == 0)
    def _():
        start_gather(0, 0)

    @pl.when(live_ref[i] > 0)
    def _():
        @pl.when((i == 0) | (te_ref[i] != te_ref[prev]))
        def _():
            for r0 in range(0, wug_ref.shape[1], CAST_ROWS):
                wug_bf[r0:r0 + CAST_ROWS, :] = wug_ref[0, r0:r0 + CAST_ROWS, :].astype(BF16)
            for r0 in range(0, wd_ref.shape[1], CAST_ROWS):
                wd_bf[r0:r0 + CAST_ROWS, :] = wd_ref[0, r0:r0 + CAST_ROWS, :].astype(BF16)

        wait_gather(slot)
        start_gather(i + 1, 1 - slot)
        gu = jnp.dot(xbuf[slot].astype(BF16), wug_bf[...], preferred_element_type=F32)
        gate, up = gu[:, :ff], gu[:, ff:]
        act = (gate * jax.nn.sigmoid(gate) * up).astype(BF16)
        o_ref[...] = jnp.dot(act, wd_bf[...], preferred_element_type=F32)

    @pl.when(live_ref[i] == 0)
    def _():
        @pl.when((i > 0) & (live_ref[prev] > 0))
        def _():
            wait_gather(slot)
        o_ref[...] = jnp.zeros_like(o_ref)


def _experts(h, plan, wug, wd, tile):
    tile_expert, live, src_tok, _ = plan
    d, ff = wd.shape[2], wd.shape[1]
    n_tiles = tile_expert.shape[0]
    grid_spec = pltpu.PrefetchScalarGridSpec(
        num_scalar_prefetch=3,
        grid=(n_tiles,),
        in_specs=[pl.BlockSpec(memory_space=pl.ANY),
                  pl.BlockSpec((1, d, 2 * ff), lambda i, te, lv, s: (te[i], 0, 0)),
                  pl.BlockSpec((1, ff, d), lambda i, te, lv, s: (te[i], 0, 0))],
        out_specs=pl.BlockSpec((tile, d), lambda i, te, lv, s: (i, 0)),
        scratch_shapes=[pltpu.VMEM((2, tile, d), F32),
                        pltpu.VMEM((d, 2 * ff), BF16), pltpu.VMEM((ff, d), BF16),
                        pltpu.SemaphoreType.DMA((2,))],
    )
    return pl.pallas_call(
        functools.partial(_expert_kernel, tile=tile, ff=ff),
        out_shape=jax.ShapeDtypeStruct((n_tiles * tile, d), F32),
        grid_spec=grid_spec,
        compiler_params=_cparams("arbitrary"),
        name="experts",
    )(tile_expert, live, src_tok, h, wug, wd)


def _combine_kernel(pos_ref, x_ref, wts_ref, g_ref, y_hbm, *rest, emit_x, n_tokens):
    *out_refs, ybuf, sem = rest
    tm = x_ref.shape[0]
    i = pl.program_id(0)
    slot = i % 2

    def start_gather(t, s):
        for static_s in range(2):
            @pl.when(s == static_s)
            def _():
                for r in range(tm):
                    for k in range(TOP_K):
                        pltpu.make_async_copy(y_hbm.at[pl.ds(pos_ref[k * n_tokens + t * tm + r], 1)],
                                              ybuf.at[static_s, k, pl.ds(r, 1)], sem.at[static_s]).start()

    @pl.when(i == 0)
    def _():
        start_gather(0, 0)

    for k in range(TOP_K):
        pltpu.make_async_copy(y_hbm.at[pl.ds(0, tm)], ybuf.at[slot, k], sem.at[slot]).wait()

    @pl.when(i + 1 < pl.num_programs(0))
    def _():
        start_gather(i + 1, 1 - slot)

    wts = wts_ref[...]
    x_new = x_ref[...] + wts[:, 0:1] * ybuf[slot, 0] + wts[:, 1:2] * ybuf[slot, 1]
    if emit_x:
        out_refs[0][...] = x_new
    out_refs[-1][...] = _rms(x_new, g_ref[...]).astype(out_refs[-1].dtype)


def _combine(x, y_sorted, pos, wts, g, emit_x, norm_dtype, tm):
    n, d = x.shape
    tm = min(tm, n)
    row = pl.BlockSpec((tm, d), lambda i, p: (i, 0))
    out_shape = [jax.ShapeDtypeStruct((n, d), norm_dtype)]
    out_specs = [row]
    if emit_x:
        out_shape.insert(0, jax.ShapeDtypeStruct((n, d), F32))
        out_specs.insert(0, row)
    grid_spec = pltpu.PrefetchScalarGridSpec(
        num_scalar_prefetch=1,
        grid=(n // tm,),
        in_specs=[row, pl.BlockSpec((tm, LANES), lambda i, p: (i, 0)), pl.BlockSpec((1, d), lambda i, p: (0, 0)),
                  pl.BlockSpec(memory_space=pl.ANY)],
        out_specs=tuple(out_specs),
        scratch_shapes=[pltpu.VMEM((2, TOP_K, tm, d), F32), pltpu.SemaphoreType.DMA((2,))],
    )
    return pl.pallas_call(
        functools.partial(_combine_kernel, emit_x=emit_x, n_tokens=n),
        out_shape=tuple(out_shape),
        grid_spec=grid_spec,
        compiler_params=_cparams("arbitrary"),
        name="combine",
    )(pos, x, wts, g.reshape(1, d), y_sorted)


def _hier_moe(routed, w_up_gate, w_down, g_next, emit_x, norm_dtype):
    x_mid, h, ids, wts = routed
    n_experts = w_up_gate.shape[0]
    plan = _moe_plan(ids, n_experts, MOE_TILE)
    y_sorted = _experts(h, plan, w_up_gate, w_down, MOE_TILE)
    return _combine(x_mid, y_sorted, plan[3], wts, g_next, emit_x, norm_dtype, tm=256)


def _rope_tables(seq, rope_dim):
    pos = jnp.arange(seq, dtype=F32)
    inv_freq = ROPE_THETA ** (-jnp.arange(0, rope_dim, 2, dtype=F32) / rope_dim)
    ang = pos[:, None] * inv_freq[None, :]
    cos, sin = jnp.cos(ang), jnp.sin(ang)
    zero = jnp.zeros_like(cos)
    return (jnp.concatenate([cos, cos, zero, zero], axis=1),
            jnp.concatenate([-sin, sin, zero, zero], axis=1))


def kernel(x, l0_g_mix, l0_w_in, l0_conv_w, l0_g_q_lat, l0_w_uq, l0_g_kv_lat, l0_w_ukv, l0_w_out, l0_g_ffn, l0_w_group, l0_b_group, l0_w_expert, l0_b_expert, l0_w_up_gate, l0_w_down, l1_g_mix, l1_w_qkv, l1_lam_q1, l1_lam_k1, l1_lam_q2, l1_lam_k2, l1_g_subln, l1_w_out, l1_g_ffn, l1_w_group, l1_b_group, l1_w_expert, l1_b_expert, l1_w_up_gate, l1_w_down, g_final):
    batch, seq, d = x.shape
    n = batch * seq
    x2 = x.reshape(n, d)
    cw = l0_conv_w.shape[1]
    qr, kvr = l0_g_q_lat.shape[0], l0_g_kv_lat.shape[0]
    heads = (d - cw) // LANES
    rope = l0_w_in.shape[1] - 3 * cw - qr - kvr
    nope = l0_w_uq.shape[1] // heads - rope
    assert rope == LANES // 2 and nope == LANES

    half = rope // 2
    kr0 = 3 * cw + qr + kvr
    t1, t2 = l0_w_in[:, kr0:kr0 + half], l0_w_in[:, kr0 + half:kr0 + rope]
    used_cols = kr0 + 2 * rope
    in_width = -(-used_cols // 512) * 512
    w_in = jnp.concatenate([l0_w_in[:, :kr0], t1, t2, t2, t1, jnp.zeros((d, in_width - used_cols), F32)], axis=1)
    proj = _norm_matmul(x2, l0_g_mix, w_in.astype(BF16), F32, tm=1024, tn=512)
    y_conv = _gated_conv(proj, l0_conv_w, seq, tm=512)

    scale_mla = (nope + rope) ** -0.5
    wq = l0_w_uq.reshape(qr, heads, nope + rope) * scale_mla
    wq = jnp.concatenate([wq[..., :nope], wq[..., nope:nope + half], wq[..., nope + half:],
                          wq[..., nope + half:], wq[..., nope:nope + half]], axis=-1)
    wq = wq.reshape(qr, heads * 2 * LANES).astype(BF16)
    cos_t, sin_t = _rope_tables(seq, rope)
    q, k, v = _mla_prep(proj, 3 * cw, l0_g_q_lat, l0_g_kv_lat, wq, l0_w_ukv.astype(BF16), cos_t, sin_t,
                        heads, seq, tm=256)
    y_mla = _mla_flash(q, k, v, batch, seq, tq=1024)
    n_experts = l0_w_expert.shape[1]
    routed = _outproj(y_conv, y_mla, 0, l0_w_out.astype(BF16), x2, l0_g_ffn,
                      *_router_params(l0_w_group, l0_b_group, l0_w_expert, l0_b_expert), n_experts, tm=256)
    x1, h1 = _hier_moe(routed, l0_w_up_gate, l0_w_down, l1_g_mix, True, BF16)

    dv = l1_g_subln.shape[0]
    dheads = d // dv
    dh = dv // 2
    lam_init = 0.8 - 0.6 * math.exp(-0.3 * 1)
    lam = (jnp.exp(jnp.sum(l1_lam_q1 * l1_lam_k1)) - jnp.exp(jnp.sum(l1_lam_q2 * l1_lam_k2)) + lam_init)
    w_qkv = jnp.concatenate([l1_w_qkv[:, :d] * dh ** -0.5, l1_w_qkv[:, d:]], axis=1).astype(BF16)
    qkv = _matmul(h1, w_qkv, BF16, tm=1024, tn=512)
    o = _diff_flash(qkv, lam, l1_g_subln, batch, seq, dheads, 1.0 - lam_init, tq=1024)
    routed = _outproj(o, o, 1, l1_w_out.astype(BF16), x1, l1_g_ffn,
                      *_router_params(l1_w_group, l1_b_group, l1_w_expert, l1_b_expert), n_experts, tm=256)
    (out,) = _hier_moe(routed, l1_w_up_gate, l1_w_down, g_final, False, F32)
    return out.reshape(batch, seq, d)
```

```python
import functools
import math

import jax
import jax.numpy as jnp
import numpy as np
from jax import lax
from jax.experimental import pallas as pl
from jax.experimental.pallas import tpu as pltpu

F32 = jnp.float32
BF16 = jnp.bfloat16

NORM_EPS = 1e-6
ROPE_THETA = 10000.0
LANES = 128
SUBLANES = 8
MOE_TILE = 256
NORM_ROWS = 256
CAST_ROWS = 256
N_GROUPS = 4
TOP_K = 2
VMEM_LIMIT_BYTES = 52 * 1024 * 1024


def _cparams(*sem):
    return pltpu.CompilerParams(dimension_semantics=sem, vmem_limit_bytes=VMEM_LIMIT_BYTES)


def _rms(xf, g):
    ms = jnp.mean(xf * xf, axis=-1, keepdims=True)
    return xf * lax.rsqrt(ms + NORM_EPS) * g


def _norm_matmul_kernel(x_ref, g_ref, w_ref, o_ref, h_ref):
    @pl.when(pl.program_id(1) == 0)
    def _():
        rows = min(NORM_ROWS, x_ref.shape[0])
        for r0 in range(0, x_ref.shape[0], rows):
            h_ref[r0:r0 + rows, :] = _rms(x_ref[r0:r0 + rows, :], g_ref[...]).astype(BF16)

    o_ref[...] = jnp.dot(h_ref[...], w_ref[...], preferred_element_type=F32).astype(o_ref.dtype)


def _norm_matmul(x, g, w, out_dtype, tm, tn):
    n, k = x.shape
    nn = w.shape[1]
    tm, tn = min(tm, n), min(tn, nn)
    return pl.pallas_call(
        _norm_matmul_kernel,
        out_shape=jax.ShapeDtypeStruct((n, nn), out_dtype),
        grid=(n // tm, nn // tn),
        in_specs=[pl.BlockSpec((tm, k), lambda i, j: (i, 0)),
                  pl.BlockSpec((1, k), lambda i, j: (0, 0)),
                  pl.BlockSpec((k, tn), lambda i, j: (0, j))],
        out_specs=pl.BlockSpec((tm, tn), lambda i, j: (i, j)),
        scratch_shapes=[pltpu.VMEM((tm, k), BF16)],
        compiler_params=_cparams("parallel", "arbitrary"),
        name="norm_matmul",
    )(x, g.reshape(1, k), w)


def _matmul_kernel(x_ref, w_ref, o_ref):
    o_ref[...] = jnp.dot(x_ref[...], w_ref[...], preferred_element_type=F32).astype(o_ref.dtype)


def _matmul(x, w, out_dtype, tm, tn):
    n, k = x.shape
    nn = w.shape[1]
    tm, tn = min(tm, n), min(tn, nn)
    return pl.pallas_call(
        _matmul_kernel,
        out_shape=jax.ShapeDtypeStruct((n, nn), out_dtype),
        grid=(n // tm, nn // tn),
        in_specs=[pl.BlockSpec((tm, k), lambda i, j: (i, 0)),
                  pl.BlockSpec((k, tn), lambda i, j: (0, j))],
        out_specs=pl.BlockSpec((tm, tn), lambda i, j: (i, j)),
        compiler_params=_cparams("parallel", "arbitrary"),
        name="matmul",
    )(x, w)


def _outproj_kernel(ya_ref, yb_ref, w_ref, x_ref, g_ref, wr_ref, br_ref, o_ref, h_ref, ids_ref, wts_ref,
                    *, n_groups, n_experts):
    ka = ya_ref.shape[1]
    acc = jnp.dot(ya_ref[...], w_ref[:ka, :], preferred_element_type=F32)
    acc = acc + jnp.dot(yb_ref[...], w_ref[ka:, :], preferred_element_type=F32)
    x_mid = x_ref[...] + acc
    o_ref[...] = x_mid
    _route(x_mid, g_ref, wr_ref, br_ref, h_ref, ids_ref, wts_ref, n_groups, n_experts)


def _outproj(ya, yb, yb_col, w, x, g_ffn, w_r, b_r, n_experts, tm):
    n, d = x.shape
    ka = d // 2
    tm = min(tm, n)
    row = pl.BlockSpec((tm, d), lambda i: (i, 0))
    lane_row = pl.BlockSpec((tm, LANES), lambda i: (i, 0))
    return pl.pallas_call(
        functools.partial(_outproj_kernel, n_groups=N_GROUPS, n_experts=n_experts),
        out_shape=(jax.ShapeDtypeStruct((n, d), F32), jax.ShapeDtypeStruct((n, d), F32),
                   jax.ShapeDtypeStruct((n, LANES), jnp.int32), jax.ShapeDtypeStruct((n, LANES), F32)),
        grid=(n // tm,),
        in_specs=[pl.BlockSpec((tm, ka), lambda i: (i, 0)),
                  pl.BlockSpec((tm, ka), lambda i: (i, yb_col)),
                  pl.BlockSpec((d, d), lambda i: (0, 0)),
                  row,
                  pl.BlockSpec((1, d), lambda i: (0, 0)),
                  pl.BlockSpec((d, 2 * LANES), lambda i: (0, 0)),
                  pl.BlockSpec((1, LANES), lambda i: (0, 0))],
        out_specs=(row, row, lane_row, lane_row),
        compiler_params=_cparams("parallel"),
        name="outproj_router",
    )(ya, yb, w, x, g_ffn.reshape(1, d), w_r, b_r)


def _conv_kernel(b_ref, c_ref, u_ref, hc_ref, hu_ref, w_ref, o_ref, *, tiles_per_seq):
    tm = c_ref.shape[0]
    first = (pl.program_id(0) % tiles_per_seq) == 0
    v = c_ref[...] * u_ref[...]
    halo = jnp.where(first, 0.0, hc_ref[...] * hu_ref[...])
    row = lax.broadcasted_iota(jnp.int32, v.shape, 0)
    v1 = jnp.where(row == 0, halo[7:8, :], pltpu.roll(v, 1, axis=0))
    v2 = jnp.where(row == 0, halo[6:7, :],
                   jnp.where(row == 1, halo[7:8, :], pltpu.roll(v, 2, axis=0)))
    w = w_ref[...]
    conv = w[2:3, :] * v + w[1:2, :] * v1 + w[0:1, :] * v2
    o_ref[...] = (b_ref[...] * conv).astype(o_ref.dtype)


def _gated_conv(proj, conv_w, seq, tm):
    n = proj.shape[0]
    cw = conv_w.shape[1]
    tm = min(tm, seq)
    hb = tm // 8
    return pl.pallas_call(
        functools.partial(_conv_kernel, tiles_per_seq=seq // tm),
        out_shape=jax.ShapeDtypeStruct((n, cw), BF16),
        grid=(n // tm,),
        in_specs=[pl.BlockSpec((tm, cw), lambda i: (i, 0)),
                  pl.BlockSpec((tm, cw), lambda i: (i, 1)),
                  pl.BlockSpec((tm, cw), lambda i: (i, 2)),
                  pl.BlockSpec((8, cw), lambda i: (jnp.maximum(i * hb - 1, 0), 1)),
                  pl.BlockSpec((8, cw), lambda i: (jnp.maximum(i * hb - 1, 0), 2)),
                  pl.BlockSpec((conv_w.shape[0], cw), lambda i: (0, 0))],
        out_specs=pl.BlockSpec((tm, cw), lambda i: (i, 0)),
        compiler_params=_cparams("parallel"),
        name="gated_conv",
    )(proj, proj, proj, proj, proj, conv_w)


def _rope_lanes(a, cos_t, sin_t):
    return a * cos_t + pltpu.roll(a, LANES // 2, axis=1) * sin_t


def _mla_prep_kernel(cq_ref, ckv_ref, kr_ref, gq_ref, gkv_ref, wq_ref, wkv_ref, cos_ref, sin_ref,
                     q_ref, k_ref, v_ref, *, heads):
    cqn = _rms(cq_ref[...], gq_ref[...]).astype(BF16)
    ckvn = _rms(ckv_ref[...], gkv_ref[...]).astype(BF16)
    cos_t, sin_t = cos_ref[...], sin_ref[...]
    q = jnp.dot(cqn, wq_ref[...], preferred_element_type=F32)
    kv = jnp.dot(ckvn, wkv_ref[...], preferred_element_type=F32)
    k_rope = _rope_lanes(kr_ref[...], cos_t, sin_t).astype(BF16)
    for h in range(heads):
        c0 = h * 2 * LANES
        q_ref[h, :, :LANES] = q[:, c0:c0 + LANES].astype(BF16)
        q_ref[h, :, LANES:] = _rope_lanes(q[:, c0 + LANES:c0 + 2 * LANES], cos_t, sin_t).astype(BF16)
        k_ref[h, :, :LANES] = kv[:, c0:c0 + LANES].astype(BF16)
        k_ref[h, :, LANES:] = k_rope
        v_ref[h] = kv[:, c0 + LANES:c0 + 2 * LANES].astype(BF16)


def _mla_prep(proj, col0, g_q, g_kv, wq, wkv, cos_t, sin_t, heads, seq, tm):
    n = proj.shape[0]
    qr, kvr = g_q.shape[0], g_kv.shape[0]
    tm = min(tm, seq)
    ps = seq // tm
    assert col0 % qr == 0 and (col0 + qr) % kvr == 0 and (col0 + qr + kvr) % LANES == 0
    return pl.pallas_call(
        functools.partial(_mla_prep_kernel, heads=heads),
        out_shape=(jax.ShapeDtypeStruct((heads, n, 2 * LANES), BF16),
                   jax.ShapeDtypeStruct((heads, n, 2 * LANES), BF16),
                   jax.ShapeDtypeStruct((heads, n, LANES), BF16)),
        grid=(n // tm,),
        in_specs=[pl.BlockSpec((tm, qr), lambda i: (i, col0 // qr)),
                  pl.BlockSpec((tm, kvr), lambda i: (i, (col0 + qr) // kvr)),
                  pl.BlockSpec((tm, LANES), lambda i: (i, (col0 + qr + kvr) // LANES)),
                  pl.BlockSpec((1, qr), lambda i: (0, 0)),
                  pl.BlockSpec((1, kvr), lambda i: (0, 0)),
                  pl.BlockSpec(wq.shape, lambda i: (0, 0)),
                  pl.BlockSpec(wkv.shape, lambda i: (0, 0)),
                  pl.BlockSpec((tm, LANES), lambda i: (i % ps, 0)),
                  pl.BlockSpec((tm, LANES), lambda i: (i % ps, 0))],
        out_specs=(pl.BlockSpec((heads, tm, 2 * LANES), lambda i: (0, i, 0)),
                   pl.BlockSpec((heads, tm, 2 * LANES), lambda i: (0, i, 0)),
                   pl.BlockSpec((heads, tm, LANES), lambda i: (0, i, 0))),
        compiler_params=_cparams("parallel"),
        name="mla_prep",
    )(proj, proj, proj, g_q.reshape(1, qr), g_kv.reshape(1, kvr), wq, wkv, cos_t, sin_t)


def _online_step(s, vblk, carry):
    m, l, acc = carry
    m_new = jnp.maximum(m, jnp.max(s, axis=1, keepdims=True))
    alpha = jnp.exp(m - m_new)
    p = jnp.exp(s - m_new)
    l = alpha * l + jnp.sum(p, axis=1, keepdims=True)
    acc = alpha * acc + jnp.dot(p.astype(BF16), vblk, preferred_element_type=F32)
    return m_new, l, acc


def _qk(q, k):
    return lax.dot_general(q, k, (((1,), (1,)), ((), ())), preferred_element_type=F32)


def _softmax_init(tq, dv):
    return (jnp.full((tq, 1), -1e30, F32), jnp.zeros((tq, 1), F32), jnp.zeros((tq, dv), F32))


def _mla_flash_kernel(q_ref, k_ref, v_ref, o_ref):
    tq = q_ref.shape[1]
    qi = pl.program_id(2)
    q = q_ref[0]

    def body(j, carry):
        off = pl.multiple_of(j * tq, tq)
        return _online_step(_qk(q, k_ref[0, pl.ds(off, tq), :]), v_ref[0, pl.ds(off, tq), :], carry)

    carry = lax.fori_loop(0, qi, body, _softmax_init(tq, v_ref.shape[2]))
    off = pl.multiple_of(qi * tq, tq)
    causal = (lax.broadcasted_iota(jnp.int32, (tq, tq), 1) <= lax.broadcasted_iota(jnp.int32, (tq, tq), 0))
    s = jnp.where(causal, _qk(q, k_ref[0, pl.ds(off, tq), :]), -jnp.inf)
    _, l, acc = _online_step(s, v_ref[0, pl.ds(off, tq), :], carry)
    o_ref[...] = (acc / l).astype(o_ref.dtype)


def _mla_flash(q, k, v, batch, seq, tq):
    heads, n, dk = q.shape
    dv = v.shape[2]
    tq = min(tq, seq)
    nq = seq // tq
    return pl.pallas_call(
        _mla_flash_kernel,
        out_shape=jax.ShapeDtypeStruct((n, heads * dv), BF16),
        grid=(batch, heads, nq),
        in_specs=[pl.BlockSpec((1, tq, dk), lambda b, h, i: (h, b * nq + i, 0)),
                  pl.BlockSpec((1, seq, dk), lambda b, h, i: (h, b, 0)),
                  pl.BlockSpec((1, seq, dv), lambda b, h, i: (h, b, 0))],
        out_specs=pl.BlockSpec((tq, dv), lambda b, h, i: (b * nq + i, h)),
        compiler_params=_cparams("parallel", "parallel", "arbitrary"),
        name="mla_flash",
    )(q, k, v)


def _diff_flash_kernel(lam_ref, slopes_ref, q_ref, k_ref, v_ref, g_ref, o_ref, *, out_scale):
    tq = q_ref.shape[0]
    dh = q_ref.shape[1] // 2
    dv = v_ref.shape[1]
    qi = pl.program_id(2)
    slope = slopes_ref[0, pl.program_id(1)]
    q1, q2 = q_ref[:, :dh], q_ref[:, dh:]
    col = lax.broadcasted_iota(jnp.int32, (1, tq), 1).astype(F32)

    def scores(off, qpart, lo):
        return _qk(qpart, k_ref[pl.ds(off, tq), lo:lo + dh])

    def body(j, carry):
        c1, c2 = carry
        off = pl.multiple_of(j * tq, tq)
        bias = slope * (col + (off - qi * tq).astype(F32))
        vblk = v_ref[pl.ds(off, tq), :]
        c1 = _online_step(scores(off, q1, 0) + bias, vblk, c1)
        c2 = _online_step(scores(off, q2, dh) + bias, vblk, c2)
        return c1, c2

    c1, c2 = lax.fori_loop(0, qi, body, (_softmax_init(tq, dv), _softmax_init(tq, dv)))
    off = pl.multiple_of(qi * tq, tq)
    causal = (lax.broadcasted_iota(jnp.int32, (tq, tq), 1) <= lax.broadcasted_iota(jnp.int32, (tq, tq), 0))
    bias = slope * col
    vblk = v_ref[pl.ds(off, tq), :]
    _, l1, a1 = _online_step(jnp.where(causal, scores(off, q1, 0) + bias, -jnp.inf), vblk, c1)
    _, l2, a2 = _online_step(jnp.where(causal, scores(off, q2, dh) + bias, -jnp.inf), vblk, c2)
    o = a1 / l1 - lam_ref[0, 0] * (a2 / l2)
    o_ref[...] = (_rms(o, g_ref[...]) * out_scale).astype(o_ref.dtype)


def _diff_flash(qkv, lam, g_subln, batch, seq, heads, out_scale, tq):
    n = qkv.shape[0]
    dv = g_subln.shape[0]
    tq = min(tq, seq)
    nq = seq // tq
    slopes = jnp.asarray((2.0 ** (-8.0 * np.arange(1, heads + 1) / heads)).astype(np.float32)).reshape(1, heads)
    return pl.pallas_call(
        functools.partial(_diff_flash_kernel, out_scale=out_scale),
        out_shape=jax.ShapeDtypeStruct((n, heads * dv), BF16),
        grid=(batch, heads, nq),
        in_specs=[pl.BlockSpec(memory_space=pltpu.SMEM),
                  pl.BlockSpec(memory_space=pltpu.SMEM),
                  pl.BlockSpec((tq, dv), lambda b, h, i: (b * nq + i, h)),
                  pl.BlockSpec((seq, dv), lambda b, h, i: (b, heads + h)),
                  pl.BlockSpec((seq, dv), lambda b, h, i: (b, 2 * heads + h)),
                  pl.BlockSpec((1, dv), lambda b, h, i: (0, 0))],
        out_specs=pl.BlockSpec((tq, dv), lambda b, h, i: (b * nq + i, h)),
        compiler_params=_cparams("parallel", "parallel", "arbitrary"),
        name="diff_flash",
    )(lam.reshape(1, 1), slopes, qkv, qkv, qkv, g_subln.reshape(1, dv))


def _route(x_mid, g_ref, w_ref, b_ref, h_ref, ids_ref, wts_ref, n_groups, n_experts):
    h = _rms(x_mid, g_ref[...])
    h_ref[...] = h
    h_hi = h.astype(BF16)
    h_lo = (h - h_hi.astype(F32)).astype(BF16)
    hi_both = jnp.dot(h_hi, w_ref[...], preferred_element_type=F32)
    lo_hi = jnp.dot(h_lo, w_ref[:, :LANES], preferred_element_type=F32)
    logits = hi_both[:, :LANES] + hi_both[:, LANES:] + lo_hi + b_ref[...]
    lane = lax.broadcasted_iota(jnp.int32, logits.shape, 1).astype(F32)
    neg = -jnp.inf
    per_group = n_experts // n_groups
    gl = jnp.where(lane < n_groups, logits, neg)
    gmax = jnp.max(gl, axis=1, keepdims=True)
    g_prob = 1.0 / jnp.sum(jnp.exp(gl - gmax), axis=1, keepdims=True)
    g_idx = jnp.min(jnp.where(gl == gmax, lane, float(LANES)), axis=1, keepdims=True)
    lo = n_groups + g_idx * per_group
    el = jnp.where((lane >= lo) & (lane < lo + per_group), logits, neg)
    m1 = jnp.max(el, axis=1, keepdims=True)
    i1 = jnp.min(jnp.where(el == m1, lane, float(LANES)), axis=1, keepdims=True)
    el2 = jnp.where(lane == i1, neg, el)
    m2 = jnp.max(el2, axis=1, keepdims=True)
    i2 = jnp.min(jnp.where(el2 == m2, lane, float(LANES)), axis=1, keepdims=True)
    e2 = jnp.exp(m2 - m1)
    w1 = g_prob / (1.0 + e2)
    w2 = g_prob * e2 / (1.0 + e2)
    ids = jnp.where(lane == 0.0, i1 - n_groups, jnp.where(lane == 1.0, i2 - n_groups, 0.0))
    ids_ref[...] = ids.astype(jnp.int32)
    wts_ref[...] = jnp.where(lane == 0.0, w1, jnp.where(lane == 1.0, w2, 0.0))


def _router_params(w_group, b_group, w_expert, b_expert):
    d, n_experts = w_expert.shape
    pad = LANES - N_GROUPS - n_experts
    w_r = jnp.concatenate([w_group, w_expert, jnp.zeros((d, pad), F32)], axis=1)
    b_r = jnp.concatenate([b_group, b_expert, jnp.zeros((pad,), F32)]).reshape(1, LANES)
    w_hi = w_r.astype(BF16)
    w_lo = (w_r - w_hi.astype(F32)).astype(BF16)
    return jnp.concatenate([w_hi, w_lo], axis=1), b_r


def _moe_plan(ids, n_experts, tile):
    n = ids.shape[0]
    na = TOP_K * n
    n_tiles = na // tile + n_experts
    ef = ids[:, :TOP_K].T.reshape(na)
    onehot = (ef[:, None] == jnp.arange(n_experts, dtype=jnp.int32)[None, :]).astype(jnp.int32)
    csum = jnp.cumsum(onehot, axis=0)
    rank = jnp.sum(csum * onehot, axis=1) - 1
    counts = csum[-1]
    tiles_per = (counts + tile - 1) // tile
    tile_end = jnp.cumsum(tiles_per)
    tile_start = tile_end - tiles_per
    pos = (tile_start[ef] * tile + rank).astype(jnp.int32)
    a = jnp.arange(na, dtype=jnp.int32)
    src_tok = jnp.zeros((n_tiles * tile,), jnp.int32).at[pos].set(a % n, unique_indices=True,
                                                                   mode="promise_in_bounds")
    t = jnp.arange(n_tiles, dtype=jnp.int32)
    used = tile_end[-1]
    tile_expert = jnp.sum((tile_end[None, :] <= jnp.minimum(t, used - 1)[:, None]).astype(jnp.int32), axis=1)
    live = (t < used).astype(jnp.int32)
    return tile_expert, live, src_tok, pos


def _expert_kernel(te_ref, live_ref, src_ref, h_hbm, wug_ref, wd_ref, o_ref, xbuf, wug_bf, wd_bf, gsem, *, tile, ff):
    i = pl.program_id(0)
    slot = i % 2
    prev = jnp.maximum(i - 1, 0)

    def start_gather(t, s):
        for static_s in range(2):
            @pl.when(s == static_s)
            def _():
                for r in range(tile):
                    pltpu.make_async_copy(h_hbm.at[pl.ds(src_ref[t * tile + r], 1)],
                                          xbuf.at[static_s, pl.ds(r, 1)], gsem.at[static_s]).start()

    def wait_gather(s):
        pltpu.make_async_copy(h_hbm.at[pl.ds(0, tile)], xbuf.at[s], gsem.at[s]).wait()

    @pl.when(i == 0)
    def _():
        start_gather(0, 0)

    @pl.when(live_ref[i] > 0)
    def _():
        @pl.when((i == 0) | (te_ref[i] != te_ref[prev]))
        def _():
            for r0 in range(0, wug_ref.shape[1], CAST_ROWS):
                wug_bf[r0:r0 + CAST_ROWS, :] = wug_ref[0, r0:r0 + CAST_ROWS, :].astype(BF16)
            for r0 in range(0, wd_ref.shape[1], CAST_ROWS):
                wd_bf[r0:r0 + CAST_ROWS, :] = wd_ref[0, r0:r0 + CAST_ROWS, :].astype(BF16)

        wait_gather(slot)
        start_gather(i + 1, 1 - slot)
        gu = jnp.dot(xbuf[slot].astype(BF16), wug_bf[...], preferred_element_type=F32)
        gate, up = gu[:, :ff], gu[:, ff:]
        act = (gate * jax.nn.sigmoid(gate) * up).astype(BF16)
        o_ref[...] = jnp.dot(act, wd_bf[...], preferred_element_type=F32)

    @pl.when(live_ref[i] == 0)
    def _():
        @pl.when((i > 0) & (live_ref[prev] > 0))
        def _():
            wait_gather(slot)
        o_ref[...] = jnp.zeros_like(o_ref)


def _experts(h, plan, wug, wd, tile):
    tile_expert, live, src_tok, _ = plan
    d, ff = wd.shape[2], wd.shape[1]
    n_tiles = tile_expert.shape[0]
    grid_spec = pltpu.PrefetchScalarGridSpec(
        num_scalar_prefetch=3,
        grid=(n_tiles,),
        in_specs=[pl.BlockSpec(memory_space=pl.ANY),
                  pl.BlockSpec((1, d, 2 * ff), lambda i, te, lv, s: (te[i], 0, 0)),
                  pl.BlockSpec((1, ff, d), lambda i, te, lv, s: (te[i], 0, 0))],
        out_specs=pl.BlockSpec((tile, d), lambda i, te, lv, s: (i, 0)),
        scratch_shapes=[pltpu.VMEM((2, tile, d), F32),
                        pltpu.VMEM((d, 2 * ff), BF16), pltpu.VMEM((ff, d), BF16),
                        pltpu.SemaphoreType.DMA((2,))],
    )
    return pl.pallas_call(
        functools.partial(_expert_kernel, tile=tile, ff=ff),
        out_shape=jax.ShapeDtypeStruct((n_tiles * tile, d), F32),
        grid_spec=grid_spec,
        compiler_params=_cparams("arbitrary"),
        name="experts",
    )(tile_expert, live, src_tok, h, wug, wd)


def _combine_kernel(pos_ref, x_ref, wts_ref, g_ref, y_hbm, *rest, emit_x, n_tokens):
    *out_refs, ybuf, sem = rest
    tm = x_ref.shape[0]
    i = pl.program_id(0)
    slot = i % 2

    def start_gather(t, s):
        for static_s in range(2):
            @pl.when(s == static_s)
            def _():
                for r in range(tm):
                    for k in range(TOP_K):
                        pltpu.make_async_copy(y_hbm.at[pl.ds(pos_ref[k * n_tokens + t * tm + r], 1)],
                                              ybuf.at[static_s, k, pl.ds(r, 1)], sem.at[static_s]).start()

    @pl.when(i == 0)
    def _():
        start_gather(0, 0)

    for k in range(TOP_K):
        pltpu.make_async_copy(y_hbm.at[pl.ds(0, tm)], ybuf.at[slot, k], sem.at[slot]).wait()

    @pl.when(i + 1 < pl.num_programs(0))
    def _():
        start_gather(i + 1, 1 - slot)

    wts = wts_ref[...]
    x_new = x_ref[...] + wts[:, 0:1] * ybuf[slot, 0] + wts[:, 1:2] * ybuf[slot, 1]
    if emit_x:
        out_refs[0][...] = x_new
    out_refs[-1][...] = _rms(x_new, g_ref[...]).astype(out_refs[-1].dtype)


def _combine(x, y_sorted, pos, wts, g, emit_x, norm_dtype, tm):
    n, d = x.shape
    tm = min(tm, n)
    row = pl.BlockSpec((tm, d), lambda i, p: (i, 0))
    out_shape = [jax.ShapeDtypeStruct((n, d), norm_dtype)]
    out_specs = [row]
    if emit_x:
        out_shape.insert(0, jax.ShapeDtypeStruct((n, d), F32))
        out_specs.insert(0, row)
    grid_spec = pltpu.PrefetchScalarGridSpec(
        num_scalar_prefetch=1,
        grid=(n // tm,),
        in_specs=[row, pl.BlockSpec((tm, LANES), lambda i, p: (i, 0)), pl.BlockSpec((1, d), lambda i, p: (0, 0)),
                  pl.BlockSpec(memory_space=pl.ANY)],
        out_specs=tuple(out_specs),
        scratch_shapes=[pltpu.VMEM((2, TOP_K, tm, d), F32), pltpu.SemaphoreType.DMA((2,))],
    )
    return pl.pallas_call(
        functools.partial(_combine_kernel, emit_x=emit_x, n_tokens=n),
        out_shape=tuple(out_shape),
        grid_spec=grid_spec,
        compiler_params=_cparams("arbitrary"),
        name="combine",
    )(pos, x, wts, g.reshape(1, d), y_sorted)


def _hier_moe(routed, w_up_gate, w_down, g_next, emit_x, norm_dtype):
    x_mid, h, ids, wts = routed
    n_experts = w_up_gate.shape[0]
    plan = _moe_plan(ids, n_experts, MOE_TILE)
    y_sorted = _experts(h, plan, w_up_gate, w_down, MOE_TILE)
    return _combine(x_mid, y_sorted, plan[3], wts, g_next, emit_x, norm_dtype, tm=256)


def _rope_tables(seq, rope_dim):
    pos = jnp.arange(seq, dtype=F32)
    inv_freq = ROPE_THETA ** (-jnp.arange(0, rope_dim, 2, dtype=F32) / rope_dim)
    ang = pos[:, None] * inv_freq[None, :]
    cos, sin = jnp.cos(ang), jnp.sin(ang)
    zero = jnp.zeros_like(cos)
    return (jnp.concatenate([cos, cos, zero, zero], axis=1),
            jnp.concatenate([-sin, sin, zero, zero], axis=1))


def kernel(x, l0_g_mix, l0_w_in, l0_conv_w, l0_g_q_lat, l0_w_uq, l0_g_kv_lat, l0_w_ukv, l0_w_out, l0_g_ffn, l0_w_group, l0_b_group, l0_w_expert, l0_b_expert, l0_w_up_gate, l0_w_down, l1_g_mix, l1_w_qkv, l1_lam_q1, l1_lam_k1, l1_lam_q2, l1_lam_k2, l1_g_subln, l1_w_out, l1_g_ffn, l1_w_group, l1_b_group, l1_w_expert, l1_b_expert, l1_w_up_gate, l1_w_down, g_final):
    batch, seq, d = x.shape
    n = batch * seq
    x2 = x.reshape(n, d)
    cw = l0_conv_w.shape[1]
    qr, kvr = l0_g_q_lat.shape[0], l0_g_kv_lat.shape[0]
    heads = (d - cw) // LANES
    rope = l0_w_in.shape[1] - 3 * cw - qr - kvr
    nope = l0_w_uq.shape[1] // heads - rope
    assert rope == LANES // 2 and nope == LANES

    half = rope // 2
    kr0 = 3 * cw + qr + kvr
    t1, t2 = l0_w_in[:, kr0:kr0 + half], l0_w_in[:, kr0 + half:kr0 + rope]
    used_cols = kr0 + 2 * rope
    in_width = -(-used_cols // 512) * 512
    w_in = jnp.concatenate([l0_w_in[:, :kr0], t1, t2, t2, t1, jnp.zeros((d, in_width - used_cols), F32)], axis=1)
    proj = _norm_matmul(x2, l0_g_mix, w_in.astype(BF16), F32, tm=1024, tn=512)
    y_conv = _gated_conv(proj, l0_conv_w, seq, tm=512)

    scale_mla = (nope + rope) ** -0.5
    wq = l0_w_uq.reshape(qr, heads, nope + rope) * scale_mla
    wq = jnp.concatenate([wq[..., :nope], wq[..., nope:nope + half], wq[..., nope + half:],
                          wq[..., nope + half:], wq[..., nope:nope + half]], axis=-1)
    wq = wq.reshape(qr, heads * 2 * LANES).astype(BF16)
    cos_t, sin_t = _rope_tables(seq, rope)
    q, k, v = _mla_prep(proj, 3 * cw, l0_g_q_lat, l0_g_kv_lat, wq, l0_w_ukv.astype(BF16), cos_t, sin_t,
                        heads, seq, tm=256)
    y_mla = _mla_flash(q, k, v, batch, seq, tq=1024)
    n_experts = l0_w_expert.shape[1]
    routed = _outproj(y_conv, y_mla, 0, l0_w_out.astype(BF16), x2, l0_g_ffn,
                      *_router_params(l0_w_group, l0_b_group, l0_w_expert, l0_b_expert), n_experts, tm=256)
    x1, h1 = _hier_moe(routed, l0_w_up_gate, l0_w_down, l1_g_mix, True, BF16)

    dv = l1_g_subln.shape[0]
    dheads = d // dv
    dh = dv // 2
    lam_init = 0.8 - 0.6 * math.exp(-0.3 * 1)
    lam = (jnp.exp(jnp.sum(l1_lam_q1 * l1_lam_k1)) - jnp.exp(jnp.sum(l1_lam_q2 * l1_lam_k2)) + lam_init)
    w_qkv = jnp.concatenate([l1_w_qkv[:, :d] * dh ** -0.5, l1_w_qkv[:, d:]], axis=1).astype(BF16)
    qkv = _matmul(h1, w_qkv, BF16, tm=1024, tn=512)
    o = _diff_flash(qkv, lam, l1_g_subln, batch, seq, dheads, 1.0 - lam_init, tq=1024)
    routed = _outproj(o, o, 1, l1_w_out.astype(BF16), x1, l1_g_ffn,
                      *_router_params(l1_w_group, l1_b_group, l1_w_expert, l1_b_expert), n_experts, tm=256)
    (out,) = _hier_moe(routed, l1_w_up_gate, l1_w_down, g_final, False, F32)
    return out.reshape(batch, seq, d)
```
